```python
import math
import jax, jax.numpy as jnp
from jax import lax
import numpy as np

D_MODEL = 2048
BATCH = 2
SEQ = 16384
DEPTH = 4

CHUNK = 64
SC_WIDTH = D_MODEL // 2
SC_KERNEL = 3
CF_WIDTH = D_MODEL - SC_WIDTH
CF_KERNEL = 31
IN_PROJ_WIDTH = 3 * SC_WIDTH + 2 * CF_WIDTH
S5_GROUP = 16
S5_GROUPS = D_MODEL // S5_GROUP
S5_STATE = 64
DT_MIN = 1e-3
DT_MAX = 1e-1
N_EXPERTS = 32
TOP_K = 4
EXPERT_FF = 768
SWIGLU_LIMIT = 7.0
SWIGLU_ALPHA = 1.702
MOE_BLOCK = 128
NORM_EPS = 1e-5

kernel_name = 'hybrid_conv_s5_moe_adaln_trunk'


def rms_norm(x, g):
    xf = x.astype(jnp.float32)
    y = xf * lax.rsqrt(jnp.mean(xf * xf, axis=-1, keepdims=True) + NORM_EPS)
    return (y * g.astype(jnp.float32)).astype(x.dtype)


def layer_norm(x, g, b):
    xf = x.astype(jnp.float32)
    mu = jnp.mean(xf, axis=-1, keepdims=True)
    var = jnp.mean(jnp.square(xf - mu), axis=-1, keepdims=True)
    y = (xf - mu) * lax.rsqrt(var + NORM_EPS)
    return (y * g.astype(jnp.float32) + b.astype(jnp.float32)).astype(x.dtype)


def modulate(h, shift, scale):
    return h * (1 + scale[:, None, :]) + shift[:, None, :]


def causal_depthwise_conv(x, w):
    k = w.shape[0]
    xp = jnp.pad(x, ((0, 0), (k - 1, 0), (0, 0)))
    return lax.conv_general_dilated(xp, w[:, None, :].astype(x.dtype), (1,), 'VALID',
                                    dimension_numbers=('NWC', 'WIO', 'NWC'),
                                    feature_group_count=x.shape[-1])


def conv_mixer(u, w_in, sc_w, cf_w, cf_b, ln_g, ln_b, w_out):
    z = u @ w_in
    b_gate, c_gate, v, cf_val, cf_gate = jnp.split(
        z, [SC_WIDTH, 2 * SC_WIDTH, 3 * SC_WIDTH, 3 * SC_WIDTH + CF_WIDTH], axis=-1)
    y_a = b_gate * causal_depthwise_conv(c_gate * v, sc_w)
    g = cf_val * jax.nn.sigmoid(cf_gate)
    g = causal_depthwise_conv(g, cf_w) + cf_b
    y_b = jax.nn.silu(layer_norm(g, ln_g, ln_b))
    return jnp.concatenate([y_a, y_b], axis=-1) @ w_out


def s5_mixer(u, lam_re, lam_im, log_dt, b_re, b_im, c_re, c_im, d_skip, w_glu):
    bsz, seq, _ = u.shape
    f32 = jnp.float32
    lam_re = lam_re.astype(f32)
    lam_im = lam_im.astype(f32)
    dt = jnp.exp(log_dt.astype(f32))[:, None]
    mag = jnp.exp(lam_re * dt)
    lb_re = mag * jnp.cos(lam_im * dt)
    lb_im = mag * jnp.sin(lam_im * dt)
    den = lam_re * lam_re + lam_im * lam_im
    f_re = ((lb_re - 1) * lam_re + lb_im * lam_im) / den
    f_im = (lb_im * lam_re - (lb_re - 1) * lam_im) / den
    b_re = b_re.astype(f32)
    b_im = b_im.astype(f32)
    bb_re = f_re[..., None] * b_re - f_im[..., None] * b_im
    bb_im = f_re[..., None] * b_im + f_im[..., None] * b_re
    c_re = c_re.astype(f32)
    c_im = c_im.astype(f32)
    n_chunks = seq // CHUNK
    uc = u.astype(f32).reshape(bsz, n_chunks, CHUNK, S5_GROUPS, S5_GROUP).transpose(1, 0, 2, 3, 4)

    def combine(left, right):
        a1r, a1i, s1r, s1i = left
        a2r, a2i, s2r, s2i = right
        return (a2r * a1r - a2i * a1i, a2r * a1i + a2i * a1r,
                a2r * s1r - a2i * s1i + s2r, a2r * s1i + a2i * s1r + s2i)

    def chunk_step(state, u_blk):
        st_re, st_im = state
        bu_re = jnp.einsum('blgh,gph->blgp', u_blk, bb_re)
        bu_im = jnp.einsum('blgh,gph->blgp', u_blk, bb_im)
        a_re = jnp.broadcast_to(lb_re, bu_re.shape)
        a_im = jnp.broadcast_to(lb_im, bu_im.shape)
        p_re, p_im, s_re, s_im = lax.associative_scan(combine, (a_re, a_im, bu_re, bu_im), axis=1)
        x_re = p_re * st_re[:, None] - p_im * st_im[:, None] + s_re
        x_im = p_re * st_im[:, None] + p_im * st_re[:, None] + s_im
        y = (jnp.einsum('blgp,ghp->blgh', x_re, c_re)
             - jnp.einsum('blgp,ghp->blgh', x_im, c_im))
        return (x_re[:, -1], x_im[:, -1]), y

    init = (jnp.zeros((bsz, S5_GROUPS, S5_STATE), f32), jnp.zeros((bsz, S5_GROUPS, S5_STATE), f32))
    _, y = lax.scan(chunk_step, init, uc)
    y = y.transpose(1, 0, 2, 3, 4).reshape(bsz, seq, D_MODEL)
    y = jax.nn.gelu(y + d_skip.astype(f32) * u.astype(f32)).astype(u.dtype)
    val, gate = jnp.split(y @ w_glu, 2, axis=-1)
    return val * jax.nn.sigmoid(gate)


def moe_ffn(u, w_router, b_router, w_gate, b_gate, w_up, b_up, w_down, b_down):
    bsz, seq, d = u.shape
    xt = u.reshape(-1, d)
    n = xt.shape[0]
    n_assign = n * TOP_K
    logits = (xt @ w_router + b_router).astype(jnp.float32)
    top_logit, top_idx = lax.top_k(logits, TOP_K)
    probs = jax.nn.softmax(top_logit, axis=-1)
    e_flat = top_idx.reshape(-1).astype(jnp.int32)
    tok_flat = jnp.arange(n_assign, dtype=jnp.int32) // TOP_K
    p_flat = probs.reshape(-1)
    order = jnp.argsort(e_flat)
    e_sorted = e_flat[order]
    counts = jnp.bincount(e_flat, length=N_EXPERTS)
    padded = (counts + MOE_BLOCK - 1) // MOE_BLOCK * MOE_BLOCK
    starts = jnp.cumsum(counts) - counts
    pad_ends = jnp.cumsum(padded)
    pad_starts = pad_ends - padded
    dest = pad_starts[e_sorted] + jnp.arange(n_assign, dtype=jnp.int32) - starts[e_sorted]
    n_slots = n_assign + N_EXPERTS * MOE_BLOCK
    n_blocks = n_slots // MOE_BLOCK
    slot_tok = jnp.zeros((n_slots,), jnp.int32).at[dest].set(tok_flat[order])
    slot_w = jnp.zeros((n_slots,), jnp.float32).at[dest].set(p_flat[order])
    block_exp = jnp.minimum(
        jnp.searchsorted(pad_ends, jnp.arange(n_blocks, dtype=jnp.int32) * MOE_BLOCK, side='right'),
        N_EXPERTS - 1)

    def block_step(acc, blk):
        tok, e, wt = blk
        xb = xt[tok]
        g = jnp.minimum(xb @ w_gate[e] + b_gate[e], SWIGLU_LIMIT)
        v = jnp.clip(xb @ w_up[e] + b_up[e], -SWIGLU_LIMIT, SWIGLU_LIMIT)
        a = g * jax.nn.sigmoid(SWIGLU_ALPHA * g) * (v + 1)
        yb = (a @ w_down[e] + b_down[e]).astype(jnp.float32) * wt[:, None]
        return acc.at[tok].add(yb), None

    acc0 = jnp.zeros((n, d), jnp.float32)
    out, _ = lax.scan(block_step, acc0,
                      (slot_tok.reshape(n_blocks, MOE_BLOCK), block_exp,
                       slot_w.reshape(n_blocks, MOE_BLOCK)))
    return out.astype(u.dtype).reshape(bsz, seq, d)


def setup_inputs(seed: int = 0) -> dict:
    key = jax.random.key(seed)
    ks = iter(jax.random.split(key, 40))
    D, E, F = D_MODEL, N_EXPERTS, EXPERT_FF
    G, P, H = S5_GROUPS, S5_STATE, S5_GROUP
    n_even = (DEPTH + 1) // 2
    n_odd = DEPTH // 2

    def nrm(shape, scale):
        return jax.random.normal(next(ks), shape, jnp.float32) * scale

    return {
        'x': nrm((BATCH, SEQ, D), 1.0),
        'c': nrm((BATCH, D), 1.0),
        'ada_w': nrm((DEPTH, D, 6 * D), 0.5 * D ** -0.5),
        'ada_b': nrm((DEPTH, 6 * D), 0.02),
        'norm_mix_g': 1.0 + nrm((DEPTH, D), 0.02),
        'norm_ffn_g': 1.0 + nrm((DEPTH, D), 0.02),
        'cv_w_in': nrm((n_even, D, IN_PROJ_WIDTH), D ** -0.5),
        'cv_sc_w': nrm((n_even, SC_KERNEL, SC_WIDTH), SC_KERNEL ** -0.5),
        'cv_cf_w': nrm((n_even, CF_KERNEL, CF_WIDTH), CF_KERNEL ** -0.5),
        'cv_cf_b': nrm((n_even, CF_WIDTH), 0.02),
        'cv_ln_g': 1.0 + nrm((n_even, CF_WIDTH), 0.02),
        'cv_ln_b': nrm((n_even, CF_WIDTH), 0.02),
        'cv_w_out': nrm((n_even, D, D), D ** -0.5),
        's5_lam_re': -0.5 + nrm((n_odd, G, P), 0.01),
        's5_lam_im': jnp.pi * jnp.arange(P, dtype=jnp.float32) + nrm((n_odd, G, P), 0.01),
        's5_log_dt': jax.random.uniform(next(ks), (n_odd, G), jnp.float32,
                                        math.log(DT_MIN), math.log(DT_MAX)),
        's5_b_re': nrm((n_odd, G, P, H), (2 * H) ** -0.5),
        's5_b_im': nrm((n_odd, G, P, H), (2 * H) ** -0.5),
        's5_c_re': nrm((n_odd, G, H, P), P ** -0.5),
        's5_c_im': nrm((n_odd, G, H, P), P ** -0.5),
        's5_d': nrm((n_odd, D), 1.0),
        's5_w_glu': nrm((n_odd, D, 2 * D), D ** -0.5),
        'router_w': nrm((DEPTH, D, E), D ** -0.5),
        'router_b': nrm((DEPTH, E), 0.01),
        'exp_w_gate': nrm((DEPTH, E, D, F), D ** -0.5),
        'exp_b_gate': nrm((DEPTH, E, F), 0.01),
        'exp_w_up': nrm((DEPTH, E, D, F), D ** -0.5),
        'exp_b_up': nrm((DEPTH, E, F), 0.01),
        'exp_w_down': nrm((DEPTH, E, F, D), F ** -0.5),
        'exp_b_down': nrm((DEPTH, E, D), 0.01),
        'final_g': 1.0 + nrm((D,), 0.02),
    }


def reference(x, c, ada_w, ada_b, norm_mix_g, norm_ffn_g,
              cv_w_in, cv_sc_w, cv_cf_w, cv_cf_b, cv_ln_g, cv_ln_b, cv_w_out,
              s5_lam_re, s5_lam_im, s5_log_dt, s5_b_re, s5_b_im, s5_c_re, s5_c_im, s5_d, s5_w_glu,
              router_w, router_b, exp_w_gate, exp_b_gate, exp_w_up, exp_b_up, exp_w_down, exp_b_down,
              final_g):
    h = x
    c_act = jax.nn.silu(c)
    for layer in range(DEPTH):
        mod = c_act @ ada_w[layer] + ada_b[layer]
        sh1, sc1, g1, sh2, sc2, g2 = jnp.split(mod, 6, axis=-1)
        u = modulate(rms_norm(h, norm_mix_g[layer]), sh1, sc1)
        j = layer // 2
        if layer % 2 == 0:
            y = conv_mixer(u, cv_w_in[j], cv_sc_w[j], cv_cf_w[j], cv_cf_b[j],
                           cv_ln_g[j], cv_ln_b[j], cv_w_out[j])
        else:
            y = s5_mixer(u, s5_lam_re[j], s5_lam_im[j], s5_log_dt[j], s5_b_re[j], s5_b_im[j],
                         s5_c_re[j], s5_c_im[j], s5_d[j], s5_w_glu[j])
        h = h + g1[:, None, :] * y
        u = modulate(rms_norm(h, norm_ffn_g[layer]), sh2, sc2)
        h = h + g2[:, None, :] * moe_ffn(u, router_w[layer], router_b[layer],
                                         exp_w_gate[layer], exp_b_gate[layer],
                                         exp_w_up[layer], exp_b_up[layer],
                                         exp_w_down[layer], exp_b_down[layer])
    return rms_norm(h, final_g)
```

```python
import functools

import jax
import jax.numpy as jnp
from jax import lax
from jax.experimental import pallas as pl
from jax.experimental.pallas import tpu as pltpu

F32 = jnp.float32
BF16 = jnp.bfloat16
NORM_EPS = 1e-5
TOP_K = 4
SWIGLU_LIMIT = 7.0
SWIGLU_ALPHA = 1.702
LANES = 128
SUBLANES = 8
VMEM_LIMIT = 56 * 1024 * 1024
S5_CHUNK = 32
ROW_TILE = 256
EXPERT_TILE = 256
NEG_BIG = -3.0e38


def _cparams(sem):
    return pltpu.CompilerParams(dimension_semantics=sem, vmem_limit_bytes=VMEM_LIMIT)


def _largest_divisor(n, candidates):
    for c in candidates:
        if n % c == 0:
            return c
    return n


def _norm_mod(x, g, shift, scale):
    ms = jnp.mean(x * x, axis=-1, keepdims=True)
    return x * lax.rsqrt(ms + NORM_EPS) * g * (1.0 + scale) + shift


def _resident(shape):
    nd = len(shape)
    return pl.BlockSpec(shape, lambda *_: (0,) * nd, pipeline_mode=pl.Buffered(1))


def _mod_body(c_ref, w_ref, b_ref, o_ref):
    c = c_ref[...]
    ca = (c * jax.nn.sigmoid(c)).astype(BF16)
    o_ref[0] = jnp.dot(ca, w_ref[0].astype(BF16), preferred_element_type=F32) + b_ref[0]


def _mod_call(c, ada_w, ada_b):
    depth, d, w = ada_w.shape
    b = c.shape[0]
    c8 = jnp.zeros((SUBLANES, d), F32).at[:b].set(c)
    tn = _largest_divisor(w, (1536, 1024, 768, 512, 384, 256, 128))
    return pl.pallas_call(
        _mod_body,
        out_shape=jax.ShapeDtypeStruct((depth, SUBLANES, w), F32),
        grid=(depth, w // tn),
        in_specs=[
            pl.BlockSpec((SUBLANES, d), lambda l, j: (0, 0)),
            pl.BlockSpec((1, d, tn), lambda l, j: (l, 0, j)),
            pl.BlockSpec((1, 1, tn), lambda l, j: (l, 0, j)),
        ],
        out_specs=pl.BlockSpec((1, SUBLANES, tn), lambda l, j: (l, 0, j)),
        compiler_params=_cparams(("arbitrary", "arbitrary")),
        name="adaln_mod",
    )(c8, ada_w, ada_b.reshape(depth, 1, w))


def _conv_body(h_ref, modr_ref, ng_ref, win_ref, scw_ref, cfw_ref, cfv_ref, wout_ref, o_ref,
               gbuf, cvbuf, ycat, *, tm, sc, cf, ka, kb, ha, hb, rc):
    @pl.when(pl.program_id(1) == 0)
    def _():
        gbuf[0:hb, :] = jnp.zeros((hb, cf), F32)
        cvbuf[0:ha, :] = jnp.zeros((ha, sc), F32)

    x = h_ref[...]
    m = modr_ref[0]
    u = _norm_mod(x, ng_ref[...], m[0:1], m[1:2])
    z = jnp.dot(u.astype(BF16), win_ref[...], preferred_element_type=F32)
    b_gate = z[:, 0:sc]
    cv = z[:, sc:2 * sc] * z[:, 2 * sc:3 * sc]
    cf_val = z[:, 3 * sc:3 * sc + cf]
    cf_gate = z[:, 3 * sc + cf:3 * sc + 2 * cf]
    cvbuf[ha:ha + tm, :] = cv
    gbuf[hb:hb + tm, :] = cf_val * jax.nn.sigmoid(cf_gate)

    acc = scw_ref[ka - 1:ka, :] * cv
    for k in range(ka - 1):
        off = ha - (ka - 1) + k
        acc = acc + scw_ref[k:k + 1, :] * cvbuf[off:off + tm, :]
    ycat[:, 0:sc] = (b_gate * acc).astype(BF16)

    cf_b = cfv_ref[0:1, :]
    ln_g = cfv_ref[1:2, :]
    ln_b = cfv_ref[2:3, :]
    for r0 in range(0, tm, rc):
        a = cfw_ref[kb - 1:kb, :] * gbuf[hb + r0:hb + r0 + rc, :]
        for k in range(kb - 1):
            off = hb - (kb - 1) + k + r0
            a = a + cfw_ref[k:k + 1, :] * gbuf[off:off + rc, :]
        a = a + cf_b
        mu = jnp.mean(a, axis=-1, keepdims=True)
        ac = a - mu
        var = jnp.mean(ac * ac, axis=-1, keepdims=True)
        yn = ac * lax.rsqrt(var + NORM_EPS) * ln_g + ln_b
        ycat[r0:r0 + rc, sc:sc + cf] = (yn * jax.nn.sigmoid(yn)).astype(BF16)

    cvbuf[0:ha, :] = cvbuf[tm:tm + ha, :]
    gbuf[0:hb, :] = gbuf[tm:tm + hb, :]

    y = jnp.dot(ycat[...], wout_ref[...], preferred_element_type=F32)
    o_ref[...] = x + m[2:3] * y


def _conv_mixer_call(h, modr, ng, w_in, sc_w, cf_w, cf_b, ln_g, ln_b, w_out, bsz, seq):
    n, d = h.shape
    ka, sc = sc_w.shape
    kb, cf = cf_w.shape
    ha = -(-(ka - 1) // SUBLANES) * SUBLANES
    hb = -(-(kb - 1) // SUBLANES) * SUBLANES
    tm = min(ROW_TILE, seq)
    assert seq % tm == 0 and tm >= hb and tm >= ha
    ns = seq // tm
    scw = jnp.zeros((SUBLANES, sc), F32).at[:ka].set(sc_w)
    cfw = jnp.zeros((-(-kb // SUBLANES) * SUBLANES, cf), F32).at[:kb].set(cf_w)
    cfv = jnp.zeros((SUBLANES, cf), F32).at[0].set(cf_b).at[1].set(ln_g).at[2].set(ln_b)
    body = functools.partial(_conv_body, tm=tm, sc=sc, cf=cf, ka=ka, kb=kb, ha=ha, hb=hb,
                             rc=min(16, tm))
    return pl.pallas_call(
        body,
        out_shape=jax.ShapeDtypeStruct((n, d), F32),
        grid=(bsz, ns),
        in_specs=[
            pl.BlockSpec((tm, d), lambda b, s: (b * ns + s, 0)),
            pl.BlockSpec((1, SUBLANES, d), lambda b, s: (b, 0, 0)),
            _resident((1, d)),
            _resident(w_in.shape),
            _resident(scw.shape),
            _resident(cfw.shape),
            _resident(cfv.shape),
            _resident(w_out.shape),
        ],
        out_specs=pl.BlockSpec((tm, d), lambda b, s: (b * ns + s, 0)),
        scratch_shapes=[
            pltpu.VMEM((tm + hb, cf), F32),
            pltpu.VMEM((tm + ha, sc), F32),
            pltpu.VMEM((tm, sc + cf), BF16),
        ],
        compiler_params=_cparams(("arbitrary", "arbitrary")),
        name="conv_mixer",
    )(h, modr, ng, w_in.astype(BF16), scw, cfw, cfv, w_out.astype(BF16))


def _norm_body(h_ref, modr_ref, ng_ref, o_ref):
    m = modr_ref[0]
    o_ref[...] = _norm_mod(h_ref[...], ng_ref[...], m[0:1], m[1:2]).astype(BF16)


def _norm_call(h, modr, ng, seq):
    n, d = h.shape
    tm = min(2 * ROW_TILE, seq)
    tpb = seq // tm
    return pl.pallas_call(
        _norm_body,
        out_shape=jax.ShapeDtypeStruct((n, d), BF16),
        grid=(n // tm,),
        in_specs=[
            pl.BlockSpec((tm, d), lambda i: (i, 0)),
            pl.BlockSpec((1, SUBLANES, d), lambda i: (i // tpb, 0, 0)),
            _resident((1, d)),
        ],
        out_specs=pl.BlockSpec((tm, d), lambda i: (i, 0)),
        compiler_params=_cparams(("arbitrary",)),
        name="s5_norm",
    )(h, modr, ng)


def _s5_prep(lam_re, lam_im, log_dt, b_re, b_im, c_re, c_im, chunk):
    hi = lax.Precision.HIGHEST
    dt = jnp.exp(log_dt)[:, None]
    mag = jnp.exp(lam_re * dt)
    lb_re = mag * jnp.cos(lam_im * dt)
    lb_im = mag * jnp.sin(lam_im * dt)
    den = lam_re * lam_re + lam_im * lam_im
    f_re = ((lb_re - 1) * lam_re + lb_im * lam_im) / den
    f_im = (lb_im * lam_re - (lb_re - 1) * lam_im) / den
    bb_re = f_re[..., None] * b_re - f_im[..., None] * b_im
    bb_im = f_re[..., None] * b_im + f_im[..., None] * b_re
    g, p, hh = bb_re.shape
    prs, pis = [jnp.ones_like(lb_re)], [jnp.zeros_like(lb_im)]
    for _ in range(chunk):
        pr, pi = prs[-1], pis[-1]
        prs.append(pr * lb_re - pi * lb_im)
        pis.append(pr * lb_im + pi * lb_re)
    pr = jnp.stack(prs)
    pi = jnp.stack(pis)
    kmat = (jnp.einsum('ghp,tgp,gpk->tghk', c_re, pr[:chunk], bb_re, precision=hi)
            - jnp.einsum('ghp,tgp,gpk->tghk', c_re, pi[:chunk], bb_im, precision=hi)
            - jnp.einsum('ghp,tgp,gpk->tghk', c_im, pr[:chunk], bb_im, precision=hi)
            - jnp.einsum('ghp,tgp,gpk->tghk', c_im, pi[:chunk], bb_re, precision=hi))
    s_idx = jnp.arange(chunk)[:, None]
    t_idx = jnp.arange(chunk)[None, :]
    lag = jnp.maximum(t_idx - s_idx, 0)
    kt = jnp.where((t_idx >= s_idx)[:, :, None, None, None], kmat[lag], 0.0)
    tmat = kt.transpose(2, 0, 4, 1, 3).reshape(g, chunk * hh, chunk * hh)
    prr = pr[chunk - 1::-1][:chunk]
    pir = pi[chunk - 1::-1][:chunk]
    w_re = prr[..., None] * bb_re[None] - pir[..., None] * bb_im[None]
    w_im = prr[..., None] * bb_im[None] + pir[..., None] * bb_re[None]
    w_in = jnp.concatenate([w_re.transpose(1, 0, 3, 2), w_im.transpose(1, 0, 3, 2)], axis=-1)
    w_in = w_in.reshape(g, chunk * hh, 2 * p)
    p1r = pr[1:chunk + 1][:, :, None, :]
    p1i = pi[1:chunk + 1][:, :, None, :]
    co_r = c_re[None] * p1r - c_im[None] * p1i
    co_i = -c_re[None] * p1i - c_im[None] * p1r
    w_out = jnp.concatenate([co_r.transpose(1, 3, 0, 2), co_i.transpose(1, 3, 0, 2)], axis=1)
    w_out = w_out.reshape(g, 2 * p, chunk * hh)
    return tmat, w_in, w_out, pr[chunk], pi[chunk]


def _s5_state_body(u_ref, w_ref, o_ref, *, gb):
    for i in range(gb):
        o_ref[i] = jnp.dot(u_ref[i], w_ref[i], preferred_element_type=F32)


def _s5_state_call(ug, w_in):
    g, nc, lh = ug.shape
    p2 = w_in.shape[-1]
    gb = _largest_divisor(g, (8, 4, 2, 1))
    return pl.pallas_call(
        functools.partial(_s5_state_body, gb=gb),
        out_shape=jax.ShapeDtypeStruct((g, nc, p2), F32),
        grid=(g // gb,),
        in_specs=[
            pl.BlockSpec((gb, nc, lh), lambda i: (i, 0, 0)),
            pl.BlockSpec((gb, lh, p2), lambda i: (i, 0, 0)),
        ],
        out_specs=pl.BlockSpec((gb, nc, p2), lambda i: (i, 0, 0)),
        compiler_params=_cparams(("arbitrary",)),
        name="s5_chunk_state",
    )(ug, w_in)


def _s5_scan_body(s_ref, a_ref, o_ref, carry, *, cb):
    @pl.when(pl.program_id(1) == 0)
    def _():
        carry[...] = jnp.zeros_like(carry)

    ar = a_ref[0]
    ai = a_ref[1]

    def step(c, st):
        xr, xi = st
        o_ref[0, c] = xr
        o_ref[1, c] = xi
        return (ar * xr - ai * xi + s_ref[0, c], ar * xi + ai * xr + s_ref[1, c])

    xr, xi = lax.fori_loop(0, cb, step, (carry[0], carry[1]))
    carry[0] = xr
    carry[1] = xi


def _s5_scan_call(s4, a2, bsz):
    _, nc, r, _ = s4.shape
    ncb = nc // bsz
    cb = _largest_divisor(ncb, (64, 32, 16, 8, 4, 2, 1))
    nb = ncb // cb
    return pl.pallas_call(
        functools.partial(_s5_scan_body, cb=cb),
        out_shape=jax.ShapeDtypeStruct(s4.shape, F32),
        grid=(bsz, nb),
        in_specs=[
            pl.BlockSpec((2, cb, r, LANES), lambda b, j: (0, b * nb + j, 0, 0)),
            _resident(a2.shape),
        ],
        out_specs=pl.BlockSpec((2, cb, r, LANES), lambda b, j: (0, b * nb + j, 0, 0)),
        scratch_shapes=[pltpu.VMEM((2, r, LANES), F32)],
        compiler_params=_cparams(("arbitrary", "arbitrary")),
        name="s5_state_scan",
    )(s4, a2)


def _s5_out_body(u_ref, t_ref, x_ref, w_ref, o_ref, *, gb):
    for i in range(gb):
        y = jnp.dot(u_ref[i], t_ref[i], preferred_element_type=F32)
        y = y + jnp.dot(x_ref[i], w_ref[i], preferred_element_type=F32)
        o_ref[i] = y.astype(BF16)


def _s5_out_call(ug, tmat, xs, w_out):
    g, nc, lh = ug.shape
    p2 = xs.shape[-1]
    gb = _largest_divisor(g, (4, 2, 1))
    return pl.pallas_call(
        functools.partial(_s5_out_body, gb=gb),
        out_shape=jax.ShapeDtypeStruct((g, nc, lh), BF16),
        grid=(g // gb,),
        in_specs=[
            pl.BlockSpec((gb, nc, lh), lambda i: (i, 0, 0)),
            pl.BlockSpec((gb, lh, lh), lambda i: (i, 0, 0)),
            pl.BlockSpec((gb, nc, p2), lambda i: (i, 0, 0)),
            pl.BlockSpec((gb, p2, lh), lambda i: (i, 0, 0)),
        ],
        out_specs=pl.BlockSpec((gb, nc, lh), lambda i: (i, 0, 0)),
        compiler_params=_cparams(("arbitrary",)),
        name="s5_chunk_out",
    )(ug, tmat, xs, w_out)


def _s5_post_body(h_ref, y_ref, modr_ref, ng_ref, dsk_ref, wglu_ref, o_ref, *, d):
    x = h_ref[...]
    m = modr_ref[0]
    u = _norm_mod(x, ng_ref[...], m[0:1], m[1:2])
    yy = jax.nn.gelu(y_ref[...].astype(F32) + dsk_ref[...] * u)
    r = jnp.dot(yy.astype(BF16), wglu_ref[...], preferred_element_type=F32)
    o_ref[...] = x + m[2:3] * (r[:, 0:d] * jax.nn.sigmoid(r[:, d:2 * d]))


def _s5_post_call(h, y, modr, ng, d_skip, w_glu, seq):
    n, d = h.shape
    tm = min(ROW_TILE, seq)
    tpb = seq // tm
    return pl.pallas_call(
        functools.partial(_s5_post_body, d=d),
        out_shape=jax.ShapeDtypeStruct((n, d), F32),
        grid=(n // tm,),
        in_specs=[
            pl.BlockSpec((tm, d), lambda i: (i, 0)),
            pl.BlockSpec((tm, d), lambda i: (i, 0)),
            pl.BlockSpec((1, SUBLANES, d), lambda i: (i // tpb, 0, 0)),
            _resident((1, d)),
            _resident((1, d)),
            _resident(w_glu.shape),
        ],
        out_specs=pl.BlockSpec((tm, d), lambda i: (i, 0)),
        compiler_params=_cparams(("arbitrary",)),
        name="s5_glu",
    )(h, y, modr, ng, d_skip, w_glu.astype(BF16))


def _s5_mixer(h, modr, ng, lam_re, lam_im, log_dt, b_re, b_im, c_re, c_im, d_skip, w_glu,
              bsz, seq):
    n, d = h.shape
    g, p, hh = b_re.shape
    chunk = min(S5_CHUNK, seq)
    nc = n // chunk
    tmat, w_in, w_out, a_re, a_im = _s5_prep(lam_re, lam_im, log_dt, b_re, b_im, c_re, c_im, chunk)
    u16 = _norm_call(h, modr, ng, seq)
    ug = u16.reshape(nc, chunk, g, hh).transpose(2, 0, 1, 3).reshape(g, nc, chunk * hh)
    s = _s5_state_call(ug, w_in.astype(BF16))
    r = g * p // LANES
    s4 = s.reshape(g, nc, 2, p).transpose(2, 1, 0, 3).reshape(2, nc, r, LANES)
    a2 = jnp.stack([a_re, a_im]).reshape(2, r, LANES)
    xs4 = _s5_scan_call(s4, a2, bsz)
    xs = xs4.reshape(2, nc, g, p).transpose(2, 1, 0, 3).reshape(g, nc, 2 * p).astype(BF16)
    yg = _s5_out_call(ug, tmat.astype(BF16), xs, w_out.astype(BF16))
    y = yg.reshape(g, nc, chunk, hh).transpose(1, 2, 0, 3).reshape(n, d)
    return _s5_post_call(h, y, modr, ng, d_skip.reshape(1, d), w_glu, seq)


def _router_body(h_ref, modr_ref, ng_ref, whi_ref, wlo_ref, rb_ref, u_ref, meta_ref, cnt_ref,
                 carry, *, n_exp, tm):
    @pl.when(pl.program_id(0) == 0)
    def _():
        carry[...] = jnp.zeros_like(carry)

    m = modr_ref[0]
    u = _norm_mod(h_ref[...], ng_ref[...], m[3:4], m[4:5])
    u_ref[...] = u
    uh = u.astype(BF16)
    ul = (u - uh.astype(F32)).astype(BF16)
    whi = whi_ref[...]
    logits = (jnp.dot(uh, whi, preferred_element_type=F32)
              + jnp.dot(ul, whi, preferred_element_type=F32)
              + jnp.dot(uh, wlo_ref[...], preferred_element_type=F32)) + rb_ref[...]
    lane = lax.broadcasted_iota(jnp.int32, (tm, LANES), 1)
    lane_f = lane.astype(F32)
    l = jnp.where(lane < n_exp, logits, NEG_BIG)
    tops, idxs, sels = [], [], []
    for _ in range(TOP_K):
        mk = jnp.max(l, axis=-1, keepdims=True)
        ik = jnp.min(jnp.where(l == mk, lane_f, 1.0e9), axis=-1, keepdims=True)
        sk = lane_f == ik
        l = jnp.where(sk, NEG_BIG, l)
        tops.append(mk)
        idxs.append(ik)
        sels.append(sk)
    es = [jnp.exp(t - tops[0]) for t in tops]
    den = es[0]
    for e in es[1:]:
        den = den + e
    chosen = jnp.zeros((tm, LANES), F32)
    for sk in sels:
        chosen = jnp.where(sk, 1.0, chosen)
    row = lax.broadcasted_iota(jnp.int32, (tm, tm), 0)
    col = lax.broadcasted_iota(jnp.int32, (tm, tm), 1)
    tri = jnp.where(row > col, 1.0, 0.0).astype(BF16)
    rank_mat = jnp.dot(tri, chosen.astype(BF16), preferred_element_type=F32) + carry[0:1, :]
    meta = jnp.zeros((tm, LANES), F32)
    for k in range(TOP_K):
        rk = jnp.sum(jnp.where(sels[k], rank_mat, 0.0), axis=-1, keepdims=True)
        meta = jnp.where(lane == k, idxs[k], meta)
        meta = jnp.where(lane == TOP_K + k, rk, meta)
        meta = jnp.where(lane == 2 * TOP_K + k, es[k] / den, meta)
    meta_ref[...] = meta
    carry[...] = carry[...] + jnp.sum(chosen, axis=0, keepdims=True)
    cnt_ref[...] = carry[...]


def _router_call(h, modr, ng, w_router, b_router, seq):
    n, d = h.shape
    n_exp = w_router.shape[1]
    tm = min(ROW_TILE, seq)
    tpb = seq // tm
    wpad = jnp.zeros((d, LANES), F32).at[:, :n_exp].set(w_router)
    whi = wpad.astype(BF16)
    wlo = (wpad - whi.astype(F32)).astype(BF16)
    rb = jnp.zeros((1, LANES), F32).at[0, :n_exp].set(b_router)
    return pl.pallas_call(
        functools.partial(_router_body, n_exp=n_exp, tm=tm),
        out_shape=(
            jax.ShapeDtypeStruct((n, d), F32),
            jax.ShapeDtypeStruct((n, LANES), F32),
            jax.ShapeDtypeStruct((SUBLANES, LANES), F32),
        ),
        grid=(n // tm,),
        in_specs=[
            pl.BlockSpec((tm, d), lambda i: (i, 0)),
            pl.BlockSpec((1, SUBLANES, d), lambda i: (i // tpb, 0, 0)),
            _resident((1, d)),
            _resident(whi.shape),
            _resident(wlo.shape),
            _resident(rb.shape),
        ],
        out_specs=(
            pl.BlockSpec((tm, d), lambda i: (i, 0)),
            pl.BlockSpec((tm, LANES), lambda i: (i, 0)),
            pl.BlockSpec((SUBLANES, LANES), lambda i: (0, 0)),
        ),
        scratch_shapes=[pltpu.VMEM((SUBLANES, LANES), F32)],
        compiler_params=_cparams(("arbitrary",)),
        name="moe_router",
    )(h, modr, ng, whi, wlo, rb)


def _row_copy(src_ref, src_row, dst_ref, dst_row, sem):
    return pltpu.make_async_copy(src_ref.at[pl.ds(src_row, 1)], dst_ref.at[pl.ds(dst_row, 1)], sem)


def _dispatch_body(npad_ref, padrow_ref, dest_ref, u_ref, xs_ref, zrow, sem, *, td, pp):
    i = pl.program_id(0)
    zrow[...] = jnp.zeros_like(zrow)
    n_pad = jnp.clip(npad_ref[0] - i * pp, 0, pp)

    def issue(r, c):
        for k in range(TOP_K):
            _row_copy(u_ref, r, xs_ref, dest_ref[TOP_K * r + k], sem).start()
        return c

    lax.fori_loop(0, td, issue, 0)

    def issue_pad(j, c):
        _row_copy(zrow, 0, xs_ref, padrow_ref[i * pp + j], sem).start()
        return c

    lax.fori_loop(0, n_pad, issue_pad, 0)

    def wait_one(j, c):
        _row_copy(u_ref, 0, xs_ref, 0, sem).wait()
        return c

    lax.fori_loop(0, TOP_K * td + n_pad, wait_one, 0)


def _dispatch_call(u, dest, pad_rows, n_pad, n_slots):
    n, d = u.shape
    td = min(ROW_TILE, n)
    steps = n // td
    pp = -(-pad_rows.shape[0] // steps)
    return pl.pallas_call(
        functools.partial(_dispatch_body, td=td, pp=pp),
        out_shape=jax.ShapeDtypeStruct((n_slots, d), F32),
        grid_spec=pltpu.PrefetchScalarGridSpec(
            num_scalar_prefetch=2,
            grid=(steps,),
            in_specs=[
                pl.BlockSpec((TOP_K * td,), lambda i, npad, prow: (i,), memory_space=pltpu.SMEM),
                pl.BlockSpec((td, d), lambda i, npad, prow: (i, 0)),
            ],
            out_specs=pl.BlockSpec(memory_space=pl.ANY),
            scratch_shapes=[pltpu.VMEM((SUBLANES, d), F32), pltpu.SemaphoreType.DMA(())],
        ),
        compiler_params=pltpu.CompilerParams(dimension_semantics=("arbitrary",),
                                             vmem_limit_bytes=VMEM_LIMIT, has_side_effects=True),
        name="moe_dispatch",
    )(n_pad, pad_rows, dest, u)


def _expert_body(bexp_ref, nused_ref, x_ref, wg_ref, bg_ref, wu_ref, bu_ref, wd_ref, bd_ref, o_ref):
    @pl.when(pl.program_id(0) < nused_ref[0])
    def _():
        x = x_ref[...].astype(BF16)
        g = jnp.dot(x, wg_ref[0], preferred_element_type=F32) + bg_ref[0]
        g = jnp.minimum(g, SWIGLU_LIMIT)
        v = jnp.dot(x, wu_ref[0], preferred_element_type=F32) + bu_ref[0]
        v = jnp.clip(v, -SWIGLU_LIMIT, SWIGLU_LIMIT)
        a = g * jax.nn.sigmoid(SWIGLU_ALPHA * g) * (v + 1.0)
        o_ref[...] = jnp.dot(a.astype(BF16), wd_ref[0], preferred_element_type=F32) + bd_ref[0]


def _expert_call(xs, block_exp, n_used, wg, bg, wu, bu, wd, bd):
    n_slots, d = xs.shape
    n_exp, _, f = wg.shape
    te = EXPERT_TILE
    nblk = n_slots // te

    def xmap(b, be, nu):
        return (jnp.minimum(b, nu[0] - 1), 0)

    def wmap(b, be, nu):
        return (be[b], 0, 0)

    return pl.pallas_call(
        _expert_body,
        out_shape=jax.ShapeDtypeStruct((n_slots, d), F32),
        grid_spec=pltpu.PrefetchScalarGridSpec(
            num_scalar_prefetch=2,
            grid=(nblk,),
            in_specs=[
                pl.BlockSpec((te, d), xmap),
                pl.BlockSpec((1, d, f), wmap),
                pl.BlockSpec((1, 1, f), wmap),
                pl.BlockSpec((1, d, f), wmap),
                pl.BlockSpec((1, 1, f), wmap),
                pl.BlockSpec((1, f, d), wmap),
                pl.BlockSpec((1, 1, d), wmap),
            ],
            out_specs=pl.BlockSpec((te, d), xmap),
        ),
        compiler_params=_cparams(("arbitrary",)),
        name="moe_experts",
    )(block_exp, n_used, xs, wg, bg.reshape(n_exp, 1, f), wu, bu.reshape(n_exp, 1, f),
      wd, bd.reshape(n_exp, 1, d))


def _combine_body(dest_ref, h_ref, meta_ref, modr_ref, fg_ref, ys_ref, o_ref, buf, sem, *, tc, final):
    def issue(r, c):
        for k in range(TOP_K):
            pltpu.make_async_copy(ys_ref.at[pl.ds(dest_ref[TOP_K * r + k], 1)],
                                  buf.at[k, pl.ds(r, 1)], sem).start()
        return c

    lax.fori_loop(0, tc, issue, 0)

    def wait_one(j, c):
        pltpu.make_async_copy(ys_ref.at[pl.ds(0, 1)], buf.at[0, pl.ds(0, 1)], sem).wait()
        return c

    lax.fori_loop(0, TOP_K * tc, wait_one, 0)

    meta = meta_ref[...]
    acc = meta[:, 2 * TOP_K:2 * TOP_K + 1] * buf[0]
    for k in range(1, TOP_K):
        acc = acc + meta[:, 2 * TOP_K + k:2 * TOP_K + k + 1] * buf[k]
    out = h_ref[...] + modr_ref[0][5:6] * acc
    if final:
        ms = jnp.mean(out * out, axis=-1, keepdims=True)
        out = out * lax.rsqrt(ms + NORM_EPS) * fg_ref[...]
    o_ref[...] = out


def _combine_call(h, meta, modr, final_g, ys, dest, seq, final):
    n, d = h.shape
    tc = min(ROW_TILE // 2, seq)
    tpb = seq // tc
    return pl.pallas_call(
        functools.partial(_combine_body, tc=tc, final=final),
        out_shape=jax.ShapeDtypeStruct((n, d), F32),
        grid=(n // tc,),
        in_specs=[
            pl.BlockSpec((TOP_K * tc,), lambda i: (i,), memory_space=pltpu.SMEM),
            pl.BlockSpec((tc, d), lambda i: (i, 0)),
            pl.BlockSpec((tc, LANES), lambda i: (i, 0)),
            pl.BlockSpec((1, SUBLANES, d), lambda i: (i // tpb, 0, 0)),
            _resident((1, d)),
            pl.BlockSpec(memory_space=pl.ANY),
        ],
        out_specs=pl.BlockSpec((tc, d), lambda i: (i, 0)),
        scratch_shapes=[pltpu.VMEM((TOP_K, tc, d), F32), pltpu.SemaphoreType.DMA(())],
        compiler_params=_cparams(("arbitrary",)),
        name="moe_combine",
    )(dest, h, meta, modr, final_g, ys)


def _moe(h, modr, ng, w_router, b_router, wg, bg, wu, bu, wd, bd, final_g, seq, final):
    n, d = h.shape
    n_exp = w_router.shape[1]
    te = EXPERT_TILE
    n_slots = n * TOP_K + n_exp * te
    nblk = n_slots // te
    u, meta, cnt = _router_call(h, modr, ng, w_router, b_router, seq)

    idx = meta[:, 0:TOP_K].astype(jnp.int32)
    rank = meta[:, TOP_K:2 * TOP_K].astype(jnp.int32)
    counts = cnt[0, :n_exp].astype(jnp.int32)
    padded = (counts + te - 1) // te * te
    pad_ends = jnp.cumsum(padded)
    pad_starts = pad_ends - padded
    dest = (pad_starts[idx] + rank).reshape(-1)
    n_used = pad_ends[-1] // te
    blk = jnp.minimum(jnp.arange(nblk, dtype=jnp.int32), n_used - 1)
    block_exp = jnp.minimum(jnp.searchsorted(pad_ends, blk * te, side='right'),
                            n_exp - 1).astype(jnp.int32)
    within = jnp.arange(te, dtype=jnp.int32)[None, :]
    n_fill = (padded - counts)[:, None]
    fill_rows = jnp.where(within < n_fill, (pad_starts + counts)[:, None] + within, n_slots)
    pad_rows = jnp.sort(fill_rows.reshape(-1)).astype(jnp.int32)
    n_pad = jnp.sum(padded - counts).astype(jnp.int32).reshape(1)

    xs = _dispatch_call(u, dest, pad_rows, n_pad, n_slots)
    ys = _expert_call(xs, block_exp, n_used.astype(jnp.int32).reshape(1),
                      wg.astype(BF16), bg, wu.astype(BF16), bu, wd.astype(BF16), bd)
    return _combine_call(h, meta, modr, final_g, ys, dest, seq, final)


def kernel(x, c, ada_w, ada_b, norm_mix_g, norm_ffn_g, cv_w_in, cv_sc_w, cv_cf_w, cv_cf_b, cv_ln_g, cv_ln_b, cv_w_out, s5_lam_re, s5_lam_im, s5_log_dt, s5_b_re, s5_b_im, s5_c_re, s5_c_im, s5_d, s5_w_glu, router_w, router_b, exp_w_gate, exp_b_gate, exp_w_up, exp_b_up, exp_w_down, exp_b_down, final_g):
    bsz, seq, d = x.shape
    depth = ada_w.shape[0]
    n = bsz * seq
    assert bsz <= SUBLANES
    h = x.reshape(n, d)
    mod = _mod_call(c, ada_w, ada_b)
    fg = final_g.reshape(1, d)
    for layer in range(depth):
        modr = jnp.zeros((bsz, SUBLANES, d), F32).at[:, :6].set(mod[layer, :bsz].reshape(bsz, 6, d))
        ng = norm_mix_g[layer].reshape(1, d)
        j = layer // 2
        if layer % 2 == 0:
            h = _conv_mixer_call(h, modr, ng, cv_w_in[j], cv_sc_w[j], cv_cf_w[j], cv_cf_b[j],
                                 cv_ln_g[j], cv_ln_b[j], cv_w_out[j], bsz, seq)
        else:
            h = _s5_mixer(h, modr, ng, s5_lam_re[j], s5_lam_im[j], s5_log_dt[j], s5_b_re[j],
                          s5_b_im[j], s5_c_re[j], s5_c_im[j], s5_d[j], s5_w_glu[j], bsz, seq)
        h = _moe(h, modr, norm_ffn_g[layer].reshape(1, d), router_w[layer], router_b[layer],
                 exp_w_gate[layer], exp_b_gate[layer], exp_w_up[layer], exp_b_up[layer],
                 exp_w_down[layer], exp_b_down[layer], fg, seq, layer == depth - 1)
    return h.reshape(bsz, seq, d)
```

```python
import functools

import jax
import jax.numpy as jnp
from jax import lax
from jax.experimental import pallas as pl
from jax.experimental.pallas import tpu as pltpu

F32 = jnp.float32
BF16 = jnp.bfloat16
NORM_EPS = 1e-5
TOP_K = 4
SWIGLU_LIMIT = 7.0
SWIGLU_ALPHA = 1.702
LANES = 128
SUBLANES = 8
VMEM_LIMIT = 56 * 1024 * 1024
S5_CHUNK = 32
ROW_TILE = 256
EXPERT_TILE = 256
DMA_UNROLL = 8
NEG_BIG = -3.0e38


def _cparams(sem):
    return pltpu.CompilerParams(dimension_semantics=sem, vmem_limit_bytes=VMEM_LIMIT)


def _largest_divisor(n, candidates):
    for c in candidates:
        if n % c == 0:
            return c
    return n


def _norm_mod(x, g, shift, scale):
    ms = jnp.mean(x * x, axis=-1, keepdims=True)
    return x * lax.rsqrt(ms + NORM_EPS) * g * (1.0 + scale) + shift


def _resident(shape):
    nd = len(shape)
    return pl.BlockSpec(shape, lambda *_: (0,) * nd, pipeline_mode=pl.Buffered(1))


def _mod_body(c_ref, w_ref, b_ref, o_ref):
    c = c_ref[...]
    ca = (c * jax.nn.sigmoid(c)).astype(BF16)
    o_ref[0] = jnp.dot(ca, w_ref[0].astype(BF16), preferred_element_type=F32) + b_ref[0]


def _mod_call(c, ada_w, ada_b):
    depth, d, w = ada_w.shape
    b = c.shape[0]
    c8 = jnp.zeros((SUBLANES, d), F32).at[:b].set(c)
    tn = _largest_divisor(w, (1536, 1024, 768, 512, 384, 256, 128))
    return pl.pallas_call(
        _mod_body,
        out_shape=jax.ShapeDtypeStruct((depth, SUBLANES, w), F32),
        grid=(depth, w // tn),
        in_specs=[
            pl.BlockSpec((SUBLANES, d), lambda l, j: (0, 0)),
            pl.BlockSpec((1, d, tn), lambda l, j: (l, 0, j)),
            pl.BlockSpec((1, 1, tn), lambda l, j: (l, 0, j)),
        ],
        out_specs=pl.BlockSpec((1, SUBLANES, tn), lambda l, j: (l, 0, j)),
        compiler_params=_cparams(("arbitrary", "arbitrary")),
        name="adaln_mod",
    )(c8, ada_w, ada_b.reshape(depth, 1, w))


def _conv_body(h_ref, modr_ref, ng_ref, win_ref, scw_ref, cfw_ref, cfv_ref, wout_ref, o_ref,
               gbuf, cvbuf, ycat, *, tm, sc, cf, ka, kb, ha, hb, rc):
    @pl.when(pl.program_id(1) == 0)
    def _():
        gbuf[0, 0:hb, :] = jnp.zeros((hb, cf), F32)
        cvbuf[0:ha, :] = jnp.zeros((ha, sc), F32)

    x = h_ref[...]
    m = modr_ref[0]
    u = _norm_mod(x, ng_ref[...], m[0:1], m[1:2])
    z = jnp.dot(u.astype(BF16), win_ref[...], preferred_element_type=F32)
    b_gate = z[:, 0:sc]
    cv = z[:, sc:2 * sc] * z[:, 2 * sc:3 * sc]
    cf_val = z[:, 3 * sc:3 * sc + cf]
    cf_gate = z[:, 3 * sc + cf:3 * sc + 2 * cf]
    cvbuf[ha:ha + tm, :] = cv
    gbuf[0, hb:hb + tm, :] = cf_val * jax.nn.sigmoid(cf_gate)
    span = tm + hb - SUBLANES
    for b in range(1, SUBLANES):
        for c0 in range(0, span, 64):
            c1 = min(c0 + 64, span)
            gbuf[b, c0:c1, :] = gbuf[0, b + c0:b + c1, :]

    acc = scw_ref[ka - 1:ka, :] * cv
    for k in range(ka - 1):
        off = ha - (ka - 1) + k
        acc = acc + scw_ref[k:k + 1, :] * cvbuf[off:off + tm, :]
    ycat[:, 0:sc] = (b_gate * acc).astype(BF16)

    cf_b = cfv_ref[0:1, :]
    ln_g = cfv_ref[1:2, :]
    ln_b = cfv_ref[2:3, :]
    for r0 in range(0, tm, rc):
        a = cfw_ref[kb - 1:kb, :] * gbuf[0, hb + r0:hb + r0 + rc, :]
        for k in range(kb - 1):
            off = hb - (kb - 1) + k
            al = off // SUBLANES * SUBLANES + r0
            a = a + cfw_ref[k:k + 1, :] * gbuf[off % SUBLANES, al:al + rc, :]
        a = a + cf_b
        mu = jnp.mean(a, axis=-1, keepdims=True)
        ac = a - mu
        var = jnp.mean(ac * ac, axis=-1, keepdims=True)
        yn = ac * lax.rsqrt(var + NORM_EPS) * ln_g + ln_b
        ycat[r0:r0 + rc, sc:sc + cf] = (yn * jax.nn.sigmoid(yn)).astype(BF16)

    cvbuf[0:ha, :] = cvbuf[tm:tm + ha, :]
    gbuf[0, 0:hb, :] = gbuf[0, tm:tm + hb, :]

    y = jnp.dot(ycat[...], wout_ref[...], preferred_element_type=F32)
    o_ref[...] = x + m[2:3] * y


def _conv_mixer_call(h, modr, ng, w_in, sc_w, cf_w, cf_b, ln_g, ln_b, w_out, bsz, seq):
    n, d = h.shape
    ka, sc = sc_w.shape
    kb, cf = cf_w.shape
    ha = -(-(ka - 1) // SUBLANES) * SUBLANES
    hb = -(-(kb - 1) // SUBLANES) * SUBLANES
    tm = min(ROW_TILE, seq)
    assert seq % tm == 0 and tm >= hb and tm >= ha
    ns = seq // tm
    scw = jnp.zeros((SUBLANES, sc), F32).at[:ka].set(sc_w)
    cfw = jnp.zeros((-(-kb // SUBLANES) * SUBLANES, cf), F32).at[:kb].set(cf_w)
    cfv = jnp.zeros((SUBLANES, cf), F32).at[0].set(cf_b).at[1].set(ln_g).at[2].set(ln_b)
    body = functools.partial(_conv_body, tm=tm, sc=sc, cf=cf, ka=ka, kb=kb, ha=ha, hb=hb,
                             rc=min(16, tm))
    return pl.pallas_call(
        body,
        out_shape=jax.ShapeDtypeStruct((n, d), F32),
        grid=(bsz, ns),
        in_specs=[
            pl.BlockSpec((tm, d), lambda b, s: (b * ns + s, 0)),
            pl.BlockSpec((1, SUBLANES, d), lambda b, s: (b, 0, 0)),
            _resident((1, d)),
            _resident(w_in.shape),
            _resident(scw.shape),
            _resident(cfw.shape),
            _resident(cfv.shape),
            _resident(w_out.shape),
        ],
        out_specs=pl.BlockSpec((tm, d), lambda b, s: (b * ns + s, 0)),
        scratch_shapes=[
            pltpu.VMEM((SUBLANES, tm + hb, cf), F32),
            pltpu.VMEM((tm + ha, sc), F32),
            pltpu.VMEM((tm, sc + cf), BF16),
        ],
        compiler_params=_cparams(("arbitrary", "arbitrary")),
        name="conv_mixer",
    )(h, modr, ng, w_in.astype(BF16), scw, cfw, cfv, w_out.astype(BF16))


def _norm_body(h_ref, modr_ref, ng_ref, o_ref):
    m = modr_ref[0]
    o_ref[...] = _norm_mod(h_ref[...], ng_ref[...], m[0:1], m[1:2]).astype(BF16)


def _norm_call(h, modr, ng, seq):
    n, d = h.shape
    tm = min(2 * ROW_TILE, seq)
    tpb = seq // tm
    return pl.pallas_call(
        _norm_body,
        out_shape=jax.ShapeDtypeStruct((n, d), BF16),
        grid=(n // tm,),
        in_specs=[
            pl.BlockSpec((tm, d), lambda i: (i, 0)),
            pl.BlockSpec((1, SUBLANES, d), lambda i: (i // tpb, 0, 0)),
            _resident((1, d)),
        ],
        out_specs=pl.BlockSpec((tm, d), lambda i: (i, 0)),
        compiler_params=_cparams(("arbitrary",)),
        name="s5_norm",
    )(h, modr, ng)


def _s5_prep(lam_re, lam_im, log_dt, b_re, b_im, c_re, c_im, chunk):
    hi = lax.Precision.HIGHEST
    dt = jnp.exp(log_dt)[:, None]
    mag = jnp.exp(lam_re * dt)
    lb_re = mag * jnp.cos(lam_im * dt)
    lb_im = mag * jnp.sin(lam_im * dt)
    den = lam_re * lam_re + lam_im * lam_im
    f_re = ((lb_re - 1) * lam_re + lb_im * lam_im) / den
    f_im = (lb_im * lam_re - (lb_re - 1) * lam_im) / den
    bb_re = f_re[..., None] * b_re - f_im[..., None] * b_im
    bb_im = f_re[..., None] * b_im + f_im[..., None] * b_re
    g, p, hh = bb_re.shape
    prs, pis = [jnp.ones_like(lb_re)], [jnp.zeros_like(lb_im)]
    for _ in range(chunk):
        pr, pi = prs[-1], pis[-1]
        prs.append(pr * lb_re - pi * lb_im)
        pis.append(pr * lb_im + pi * lb_re)
    pr = jnp.stack(prs)
    pi = jnp.stack(pis)
    kmat = (jnp.einsum('ghp,tgp,gpk->tghk', c_re, pr[:chunk], bb_re, precision=hi)
            - jnp.einsum('ghp,tgp,gpk->tghk', c_re, pi[:chunk], bb_im, precision=hi)
            - jnp.einsum('ghp,tgp,gpk->tghk', c_im, pr[:chunk], bb_im, precision=hi)
            - jnp.einsum('ghp,tgp,gpk->tghk', c_im, pi[:chunk], bb_re, precision=hi))
    s_idx = jnp.arange(chunk)[:, None]
    t_idx = jnp.arange(chunk)[None, :]
    lag = jnp.maximum(t_idx - s_idx, 0)
    kt = jnp.where((t_idx >= s_idx)[:, :, None, None, None], kmat[lag], 0.0)
    tmat = kt.transpose(2, 0, 4, 1, 3).reshape(g, chunk * hh, chunk * hh)
    prr = pr[chunk - 1::-1][:chunk]
    pir = pi[chunk - 1::-1][:chunk]
    w_re = prr[..., None] * bb_re[None] - pir[..., None] * bb_im[None]
    w_im = prr[..., None] * bb_im[None] + pir[..., None] * bb_re[None]
    w_in = jnp.concatenate([w_re.transpose(1, 0, 3, 2), w_im.transpose(1, 0, 3, 2)], axis=-1)
    w_in = w_in.reshape(g, chunk * hh, 2 * p)
    p1r = pr[1:chunk + 1][:, :, None, :]
    p1i = pi[1:chunk + 1][:, :, None, :]
    co_r = c_re[None] * p1r - c_im[None] * p1i
    co_i = -c_re[None] * p1i - c_im[None] * p1r
    w_out = jnp.concatenate([co_r.transpose(1, 3, 0, 2), co_i.transpose(1, 3, 0, 2)], axis=1)
    w_out = w_out.reshape(g, 2 * p, chunk * hh)
    return tmat, w_in, w_out, pr[chunk], pi[chunk]


def _s5_state_body(u_ref, w_ref, o_ref, *, gb):
    for i in range(gb):
        o_ref[i] = jnp.dot(u_ref[i], w_ref[i], preferred_element_type=F32)


def _s5_state_call(ug, w_in):
    g, nc, lh = ug.shape
    p2 = w_in.shape[-1]
    gb = _largest_divisor(g, (8, 4, 2, 1))
    return pl.pallas_call(
        functools.partial(_s5_state_body, gb=gb),
        out_shape=jax.ShapeDtypeStruct((g, nc, p2), F32),
        grid=(g // gb,),
        in_specs=[
            pl.BlockSpec((gb, nc, lh), lambda i: (i, 0, 0)),
            pl.BlockSpec((gb, lh, p2), lambda i: (i, 0, 0)),
        ],
        out_specs=pl.BlockSpec((gb, nc, p2), lambda i: (i, 0, 0)),
        compiler_params=_cparams(("arbitrary",)),
        name="s5_chunk_state",
    )(ug, w_in)


def _s5_scan_body(s_ref, a_ref, o_ref, carry, *, cb):
    @pl.when(pl.program_id(1) == 0)
    def _():
        carry[...] = jnp.zeros_like(carry)

    ar = a_ref[0]
    ai = a_ref[1]

    def step(c, st):
        xr, xi = st
        o_ref[0, c] = xr
        o_ref[1, c] = xi
        return (ar * xr - ai * xi + s_ref[0, c], ar * xi + ai * xr + s_ref[1, c])

    xr, xi = lax.fori_loop(0, cb, step, (carry[0], carry[1]))
    carry[0] = xr
    carry[1] = xi


def _s5_scan_call(s4, a2, bsz):
    _, nc, r, _ = s4.shape
    ncb = nc // bsz
    cb = _largest_divisor(ncb, (64, 32, 16, 8, 4, 2, 1))
    nb = ncb // cb
    return pl.pallas_call(
        functools.partial(_s5_scan_body, cb=cb),
        out_shape=jax.ShapeDtypeStruct(s4.shape, F32),
        grid=(bsz, nb),
        in_specs=[
            pl.BlockSpec((2, cb, r, LANES), lambda b, j: (0, b * nb + j, 0, 0)),
            _resident(a2.shape),
        ],
        out_specs=pl.BlockSpec((2, cb, r, LANES), lambda b, j: (0, b * nb + j, 0, 0)),
        scratch_shapes=[pltpu.VMEM((2, r, LANES), F32)],
        compiler_params=_cparams(("arbitrary", "arbitrary")),
        name="s5_state_scan",
    )(s4, a2)


def _s5_out_body(u_ref, t_ref, x_ref, w_ref, o_ref, *, gb):
    for i in range(gb):
        y = jnp.dot(u_ref[i], t_ref[i], preferred_element_type=F32)
        y = y + jnp.dot(x_ref[i], w_ref[i], preferred_element_type=F32)
        o_ref[i] = y.astype(BF16)


def _s5_out_call(ug, tmat, xs, w_out):
    g, nc, lh = ug.shape
    p2 = xs.shape[-1]
    gb = _largest_divisor(g, (4, 2, 1))
    return pl.pallas_call(
        functools.partial(_s5_out_body, gb=gb),
        out_shape=jax.ShapeDtypeStruct((g, nc, lh), BF16),
        grid=(g // gb,),
        in_specs=[
            pl.BlockSpec((gb, nc, lh), lambda i: (i, 0, 0)),
            pl.BlockSpec((gb, lh, lh), lambda i: (i, 0, 0)),
            pl.BlockSpec((gb, nc, p2), lambda i: (i, 0, 0)),
            pl.BlockSpec((gb, p2, lh), lambda i: (i, 0, 0)),
        ],
        out_specs=pl.BlockSpec((gb, nc, lh), lambda i: (i, 0, 0)),
        compiler_params=_cparams(("arbitrary",)),
        name="s5_chunk_out",
    )(ug, tmat, xs, w_out)


def _s5_post_body(h_ref, y_ref, modr_ref, ng_ref, dsk_ref, wglu_ref, o_ref, *, d):
    x = h_ref[...]
    m = modr_ref[0]
    u = _norm_mod(x, ng_ref[...], m[0:1], m[1:2])
    yy = jax.nn.gelu(y_ref[...].astype(F32) + dsk_ref[...] * u)
    r = jnp.dot(yy.astype(BF16), wglu_ref[...], preferred_element_type=F32)
    o_ref[...] = x + m[2:3] * (r[:, 0:d] * jax.nn.sigmoid(r[:, d:2 * d]))


def _s5_post_call(h, y, modr, ng, d_skip, w_glu, seq):
    n, d = h.shape
    tm = min(ROW_TILE, seq)
    tpb = seq // tm
    return pl.pallas_call(
        functools.partial(_s5_post_body, d=d),
        out_shape=jax.ShapeDtypeStruct((n, d), F32),
        grid=(n // tm,),
        in_specs=[
            pl.BlockSpec((tm, d), lambda i: (i, 0)),
            pl.BlockSpec((tm, d), lambda i: (i, 0)),
            pl.BlockSpec((1, SUBLANES, d), lambda i: (i // tpb, 0, 0)),
            _resident((1, d)),
            _resident((1, d)),
            _resident(w_glu.shape),
        ],
        out_specs=pl.BlockSpec((tm, d), lambda i: (i, 0)),
        compiler_params=_cparams(("arbitrary",)),
        name="s5_glu",
    )(h, y, modr, ng, d_skip, w_glu.astype(BF16))


def _s5_mixer(h, modr, ng, lam_re, lam_im, log_dt, b_re, b_im, c_re, c_im, d_skip, w_glu,
              bsz, seq):
    n, d = h.shape
    g, p, hh = b_re.shape
    chunk = min(S5_CHUNK, seq)
    nc = n // chunk
    tmat, w_in, w_out, a_re, a_im = _s5_prep(lam_re, lam_im, log_dt, b_re, b_im, c_re, c_im, chunk)
    u16 = _norm_call(h, modr, ng, seq)
    ug = u16.reshape(nc, chunk, g, hh).transpose(2, 0, 1, 3).reshape(g, nc, chunk * hh)
    s = _s5_state_call(ug, w_in.astype(BF16))
    r = g * p // LANES
    s4 = s.reshape(g, nc, 2, p).transpose(2, 1, 0, 3).reshape(2, nc, r, LANES)
    a2 = jnp.stack([a_re, a_im]).reshape(2, r, LANES)
    xs4 = _s5_scan_call(s4, a2, bsz)
    xs = xs4.reshape(2, nc, g, p).transpose(2, 1, 0, 3).reshape(g, nc, 2 * p).astype(BF16)
    yg = _s5_out_call(ug, tmat.astype(BF16), xs, w_out.astype(BF16))
    y = yg.reshape(g, nc, chunk, hh).transpose(1, 2, 0, 3).reshape(n, d)
    return _s5_post_call(h, y, modr, ng, d_skip.reshape(1, d), w_glu, seq)


def _router_body(h_ref, modr_ref, ng_ref, whi_ref, wlo_ref, rb_ref, u_ref, meta_ref, cnt_ref,
                 carry, *, n_exp, tm):
    @pl.when(pl.program_id(0) == 0)
    def _():
        carry[...] = jnp.zeros_like(carry)

    m = modr_ref[0]
    u = _norm_mod(h_ref[...], ng_ref[...], m[3:4], m[4:5])
    u_ref[...] = u
    uh = u.astype(BF16)
    ul = (u - uh.astype(F32)).astype(BF16)
    whi = whi_ref[...]
    logits = (jnp.dot(uh, whi, preferred_element_type=F32)
              + jnp.dot(ul, whi, preferred_element_type=F32)
              + jnp.dot(uh, wlo_ref[...], preferred_element_type=F32)) + rb_ref[...]
    lane = lax.broadcasted_iota(jnp.int32, (tm, LANES), 1)
    lane_f = lane.astype(F32)
    l = jnp.where(lane < n_exp, logits, NEG_BIG)
    tops, idxs, sels = [], [], []
    for _ in range(TOP_K):
        mk = jnp.max(l, axis=-1, keepdims=True)
        ik = jnp.min(jnp.where(l == mk, lane_f, 1.0e9), axis=-1, keepdims=True)
        sk = lane_f == ik
        l = jnp.where(sk, NEG_BIG, l)
        tops.append(mk)
        idxs.append(ik)
        sels.append(sk)
    es = [jnp.exp(t - tops[0]) for t in tops]
    den = es[0]
    for e in es[1:]:
        den = den + e
    chosen = jnp.zeros((tm, LANES), F32)
    for sk in sels:
        chosen = jnp.where(sk, 1.0, chosen)
    row = lax.broadcasted_iota(jnp.int32, (tm, tm), 0)
    col = lax.broadcasted_iota(jnp.int32, (tm, tm), 1)
    tri = jnp.where(row > col, 1.0, 0.0).astype(BF16)
    rank_mat = jnp.dot(tri, chosen.astype(BF16), preferred_element_type=F32) + carry[0:1, :]
    meta = jnp.zeros((tm, LANES), F32)
    for k in range(TOP_K):
        rk = jnp.sum(jnp.where(sels[k], rank_mat, 0.0), axis=-1, keepdims=True)
        meta = jnp.where(lane == k, idxs[k], meta)
        meta = jnp.where(lane == TOP_K + k, rk, meta)
        meta = jnp.where(lane == 2 * TOP_K + k, es[k] / den, meta)
    meta_ref[...] = meta
    carry[...] = carry[...] + jnp.sum(chosen, axis=0, keepdims=True)
    cnt_ref[...] = carry[...]


def _router_call(h, modr, ng, w_router, b_router, seq):
    n, d = h.shape
    n_exp = w_router.shape[1]
    tm = min(ROW_TILE, seq)
    tpb = seq // tm
    wpad = jnp.zeros((d, LANES), F32).at[:, :n_exp].set(w_router)
    whi = wpad.astype(BF16)
    wlo = (wpad - whi.astype(F32)).astype(BF16)
    rb = jnp.zeros((1, LANES), F32).at[0, :n_exp].set(b_router)
    return pl.pallas_call(
        functools.partial(_router_body, n_exp=n_exp, tm=tm),
        out_shape=(
            jax.ShapeDtypeStruct((n, d), F32),
            jax.ShapeDtypeStruct((n, LANES), F32),
            jax.ShapeDtypeStruct((SUBLANES, LANES), F32),
        ),
        grid=(n // tm,),
        in_specs=[
            pl.BlockSpec((tm, d), lambda i: (i, 0)),
            pl.BlockSpec((1, SUBLANES, d), lambda i: (i // tpb, 0, 0)),
            _resident((1, d)),
            _resident(whi.shape),
            _resident(wlo.shape),
            _resident(rb.shape),
        ],
        out_specs=(
            pl.BlockSpec((tm, d), lambda i: (i, 0)),
            pl.BlockSpec((tm, LANES), lambda i: (i, 0)),
            pl.BlockSpec((SUBLANES, LANES), lambda i: (0, 0)),
        ),
        scratch_shapes=[pltpu.VMEM((SUBLANES, LANES), F32)],
        compiler_params=_cparams(("arbitrary",)),
        name="moe_router",
    )(h, modr, ng, whi, wlo, rb)


def _row_copy(src_ref, src_row, dst_ref, dst_row, sem):
    return pltpu.make_async_copy(src_ref.at[pl.ds(src_row, 1)], dst_ref.at[pl.ds(dst_row, 1)], sem)


def _rows_wait(hbm_ref, rows, sem):
    pltpu.make_async_copy(hbm_ref.at[pl.ds(0, rows)], hbm_ref.at[pl.ds(0, rows)], sem).wait()


def _dispatch_body(npad_ref, padrow_ref, dest_ref, u_ref, xs_ref, zrow, sem, *, td, pp):
    i = pl.program_id(0)
    zrow[...] = jnp.zeros_like(zrow)
    n_pad = jnp.clip(npad_ref[0] - i * pp, 0, pp)

    def issue(r, c):
        for k in range(TOP_K):
            _row_copy(u_ref, r, xs_ref, dest_ref[TOP_K * r + k], sem).start()
        return c

    lax.fori_loop(0, td, issue, 0, unroll=DMA_UNROLL)

    def issue_pad(j, c):
        _row_copy(zrow, 0, xs_ref, padrow_ref[i * pp + j], sem).start()
        return c

    lax.fori_loop(0, n_pad, issue_pad, 0)

    _rows_wait(xs_ref, TOP_K * td, sem)

    def wait_one(j, c):
        _row_copy(zrow, 0, xs_ref, 0, sem).wait()
        return c

    lax.fori_loop(0, n_pad, wait_one, 0)


def _dispatch_call(u, dest, pad_rows, n_pad, n_slots):
    n, d = u.shape
    td = min(ROW_TILE, n)
    steps = n // td
    pp = -(-pad_rows.shape[0] // steps)
    return pl.pallas_call(
        functools.partial(_dispatch_body, td=td, pp=pp),
        out_shape=jax.ShapeDtypeStruct((n_slots, d), F32),
        grid_spec=pltpu.PrefetchScalarGridSpec(
            num_scalar_prefetch=2,
            grid=(steps,),
            in_specs=[
                pl.BlockSpec((TOP_K * td,), lambda i, npad, prow: (i,), memory_space=pltpu.SMEM),
                pl.BlockSpec((td, d), lambda i, npad, prow: (i, 0)),
            ],
            out_specs=pl.BlockSpec(memory_space=pl.ANY),
            scratch_shapes=[pltpu.VMEM((SUBLANES, d), F32), pltpu.SemaphoreType.DMA(())],
        ),
        compiler_params=pltpu.CompilerParams(dimension_semantics=("arbitrary",),
                                             vmem_limit_bytes=VMEM_LIMIT, has_side_effects=True,
                                             disable_bounds_checks=True),
        name="moe_dispatch",
    )(n_pad, pad_rows, dest, u)


def _expert_body(bexp_ref, nused_ref, x_ref, wg_ref, bg_ref, wu_ref, bu_ref, wd_ref, bd_ref, o_ref):
    @pl.when(pl.program_id(0) < nused_ref[0])
    def _():
        x = x_ref[...].astype(BF16)
        g = jnp.dot(x, wg_ref[0, 0], preferred_element_type=F32) + bg_ref[0, 0]
        g = jnp.minimum(g, SWIGLU_LIMIT)
        v = jnp.dot(x, wu_ref[0, 0], preferred_element_type=F32) + bu_ref[0, 0]
        v = jnp.clip(v, -SWIGLU_LIMIT, SWIGLU_LIMIT)
        a = g * jax.nn.sigmoid(SWIGLU_ALPHA * g) * (v + 1.0)
        o_ref[...] = jnp.dot(a.astype(BF16), wd_ref[0, 0], preferred_element_type=F32) + bd_ref[0, 0]


def _expert_call(xs, block_exp, n_used, layer, wg, bg, wu, bu, wd, bd):
    n_slots, d = xs.shape
    depth, n_exp, _, f = wg.shape
    te = EXPERT_TILE
    nblk = n_slots // te

    def xmap(b, be, nu):
        return (jnp.minimum(b, nu[0] - 1), 0)

    def wmap(b, be, nu):
        return (layer, be[b], 0, 0)

    return pl.pallas_call(
        _expert_body,
        out_shape=jax.ShapeDtypeStruct((n_slots, d), F32),
        grid_spec=pltpu.PrefetchScalarGridSpec(
            num_scalar_prefetch=2,
            grid=(nblk,),
            in_specs=[
                pl.BlockSpec((te, d), xmap),
                pl.BlockSpec((1, 1, d, f), wmap),
                pl.BlockSpec((1, 1, 1, f), wmap),
                pl.BlockSpec((1, 1, d, f), wmap),
                pl.BlockSpec((1, 1, 1, f), wmap),
                pl.BlockSpec((1, 1, f, d), wmap),
                pl.BlockSpec((1, 1, 1, d), wmap),
            ],
            out_specs=pl.BlockSpec((te, d), xmap),
        ),
        compiler_params=_cparams(("arbitrary",)),
        name="moe_experts",
    )(block_exp, n_used, xs, wg, bg.reshape(depth, n_exp, 1, f), wu, bu.reshape(depth, n_exp, 1, f),
      wd, bd.reshape(depth, n_exp, 1, d))


def _combine_body(dest_ref, dnext_ref, h_ref, meta_ref, modr_ref, fg_ref, ys_ref, o_ref, buf, sem,
                  *, tc, final):
    i = pl.program_id(0)
    slot = i % 2

    def gather(d_ref, s):
        def issue(r, c):
            for k in range(TOP_K):
                pltpu.make_async_copy(ys_ref.at[pl.ds(d_ref[TOP_K * r + k], 1)],
                                      buf.at[s, k, pl.ds(r, 1)], sem.at[s]).start()
            return c

        lax.fori_loop(0, tc, issue, 0, unroll=DMA_UNROLL)

    @pl.when(i == 0)
    def _():
        gather(dest_ref, 0)

    @pl.when(i + 1 < pl.num_programs(0))
    def _():
        gather(dnext_ref, 1 - slot)

    _rows_wait(ys_ref, TOP_K * tc, sem.at[slot])

    meta = meta_ref[...]
    acc = meta[:, 2 * TOP_K:2 * TOP_K + 1] * buf[slot, 0]
    for k in range(1, TOP_K):
        acc = acc + meta[:, 2 * TOP_K + k:2 * TOP_K + k + 1] * buf[slot, k]
    out = h_ref[...] + modr_ref[0][5:6] * acc
    if final:
        ms = jnp.mean(out * out, axis=-1, keepdims=True)
        out = out * lax.rsqrt(ms + NORM_EPS) * fg_ref[...]
    o_ref[...] = out


def _combine_call(h, meta, modr, final_g, ys, dest, seq, final):
    n, d = h.shape
    tc = min(ROW_TILE // 2, seq)
    tpb = seq // tc
    steps = n // tc
    return pl.pallas_call(
        functools.partial(_combine_body, tc=tc, final=final),
        out_shape=jax.ShapeDtypeStruct((n, d), F32),
        grid=(steps,),
        in_specs=[
            pl.BlockSpec((TOP_K * tc,), lambda i: (i,), memory_space=pltpu.SMEM),
            pl.BlockSpec((TOP_K * tc,), lambda i: (jnp.minimum(i + 1, steps - 1),),
                         memory_space=pltpu.SMEM),
            pl.BlockSpec((tc, d), lambda i: (i, 0)),
            pl.BlockSpec((tc, LANES), lambda i: (i, 0)),
            pl.BlockSpec((1, SUBLANES, d), lambda i: (i // tpb, 0, 0)),
            _resident((1, d)),
            pl.BlockSpec(memory_space=pl.ANY),
        ],
        out_specs=pl.BlockSpec((tc, d), lambda i: (i, 0)),
        scratch_shapes=[pltpu.VMEM((2, TOP_K, tc, d), F32), pltpu.SemaphoreType.DMA((2,))],
        compiler_params=pltpu.CompilerParams(dimension_semantics=("arbitrary",),
                                             vmem_limit_bytes=VMEM_LIMIT,
                                             disable_bounds_checks=True),
        name="moe_combine",
    )(dest, dest, h, meta, modr, final_g, ys)


def _moe(h, modr, ng, w_router, b_router, layer, wg, bg, wu, bu, wd, bd, final_g, seq, final):
    n, d = h.shape
    n_exp = w_router.shape[1]
    te = EXPERT_TILE
    n_slots = n * TOP_K + n_exp * te
    nblk = n_slots // te
    u, meta, cnt = _router_call(h, modr, ng, w_router, b_router, seq)

    idx = meta[:, 0:TOP_K].astype(jnp.int32)
    rank = meta[:, TOP_K:2 * TOP_K].astype(jnp.int32)
    counts = cnt[0, :n_exp].astype(jnp.int32)
    padded = (counts + te - 1) // te * te
    pad_ends = jnp.cumsum(padded)
    pad_starts = pad_ends - padded
    dest = (pad_starts[idx] + rank).reshape(-1)
    n_used = pad_ends[-1] // te
    blk = jnp.minimum(jnp.arange(nblk, dtype=jnp.int32), n_used - 1)
    block_exp = jnp.minimum(jnp.sum(pad_ends[None, :] <= (blk * te)[:, None], axis=1),
                            n_exp - 1).astype(jnp.int32)
    within = jnp.arange(te, dtype=jnp.int32)[None, :]
    n_fill = (padded - counts)[:, None]
    fill_rows = jnp.where(within < n_fill, (pad_starts + counts)[:, None] + within, n_slots)
    pad_rows = jnp.sort(fill_rows.reshape(-1)).astype(jnp.int32)
    n_pad = jnp.sum(padded - counts).astype(jnp.int32).reshape(1)

    xs = _dispatch_call(u, dest, pad_rows, n_pad, n_slots)
    ys = _expert_call(xs, block_exp, n_used.astype(jnp.int32).reshape(1), layer,
                      wg, bg, wu, bu, wd, bd)
    return _combine_call(h, meta, modr, final_g, ys, dest, seq, final)


def kernel(x, c, ada_w, ada_b, norm_mix_g, norm_ffn_g, cv_w_in, cv_sc_w, cv_cf_w, cv_cf_b, cv_ln_g, cv_ln_b, cv_w_out, s5_lam_re, s5_lam_im, s5_log_dt, s5_b_re, s5_b_im, s5_c_re, s5_c_im, s5_d, s5_w_glu, router_w, router_b, exp_w_gate, exp_b_gate, exp_w_up, exp_b_up, exp_w_down, exp_b_down, final_g):
    bsz, seq, d = x.shape
    depth = ada_w.shape[0]
    n = bsz * seq
    assert bsz <= SUBLANES
    h = x.reshape(n, d)
    mod = _mod_call(c, ada_w, ada_b)
    fg = final_g.reshape(1, d)
    wg16 = exp_w_gate.astype(BF16)
    wu16 = exp_w_up.astype(BF16)
    wd16 = exp_w_down.astype(BF16)
    for layer in range(depth):
        modr = jnp.zeros((bsz, SUBLANES, d), F32).at[:, :6].set(mod[layer, :bsz].reshape(bsz, 6, d))
        ng = norm_mix_g[layer].reshape(1, d)
        j = layer // 2
        if layer % 2 == 0:
            h = _conv_mixer_call(h, modr, ng, cv_w_in[j], cv_sc_w[j], cv_cf_w[j], cv_cf_b[j],
                                 cv_ln_g[j], cv_ln_b[j], cv_w_out[j], bsz, seq)
        else:
            h = _s5_mixer(h, modr, ng, s5_lam_re[j], s5_lam_im[j], s5_log_dt[j], s5_b_re[j],
                          s5_b_im[j], s5_c_re[j], s5_c_im[j], s5_d[j], s5_w_glu[j], bsz, seq)
        h = _moe(h, modr, norm_ffn_g[layer].reshape(1, d), router_w[layer], router_b[layer],
                 layer, wg16, exp_b_gate, wu16, exp_b_up, wd16, exp_b_down, fg, seq,
                 layer == depth - 1)
    return h.reshape(bsz, seq, d)
```

```python
import functools

import jax
import jax.numpy as jnp
from jax import lax
from jax.experimental import pallas as pl
from jax.experimental.pallas import tpu as pltpu

F32 = jnp.float32
BF16 = jnp.bfloat16
NORM_EPS = 1e-5
TOP_K = 4
SWIGLU_LIMIT = 7.0
SWIGLU_ALPHA = 1.702
LANES = 128
SUBLANES = 8
VMEM_LIMIT = 56 * 1024 * 1024
S5_CHUNK = 16
ROW_TILE = 256
EXPERT_TILE = 256
DMA_UNROLL = 8
NEG_BIG = -3.0e38


def _cparams(sem):
    return pltpu.CompilerParams(dimension_semantics=sem, vmem_limit_bytes=VMEM_LIMIT)


def _largest_divisor(n, candidates):
    for c in candidates:
        if n % c == 0:
            return c
    return n


def _norm_mod(x, g, shift, scale):
    ms = jnp.mean(x * x, axis=-1, keepdims=True)
    return x * lax.rsqrt(ms + NORM_EPS) * g * (1.0 + scale) + shift


def _resident(shape):
    nd = len(shape)
    return pl.BlockSpec(shape, lambda *_: (0,) * nd, pipeline_mode=pl.Buffered(1))


def _mod_body(c_ref, w_ref, b_ref, o_ref):
    c = c_ref[...]
    ca = (c * jax.nn.sigmoid(c)).astype(BF16)
    o_ref[0] = jnp.dot(ca, w_ref[0].astype(BF16), preferred_element_type=F32) + b_ref[0]


def _mod_call(c, ada_w, ada_b):
    depth, d, w = ada_w.shape
    b = c.shape[0]
    c8 = jnp.zeros((SUBLANES, d), F32).at[:b].set(c)
    tn = _largest_divisor(w, (1536, 1024, 768, 512, 384, 256, 128))
    return pl.pallas_call(
        _mod_body,
        out_shape=jax.ShapeDtypeStruct((depth, SUBLANES, w), F32),
        grid=(depth, w // tn),
        in_specs=[
            pl.BlockSpec((SUBLANES, d), lambda l, j: (0, 0)),
            pl.BlockSpec((1, d, tn), lambda l, j: (l, 0, j)),
            pl.BlockSpec((1, 1, tn), lambda l, j: (l, 0, j)),
        ],
        out_specs=pl.BlockSpec((1, SUBLANES, tn), lambda l, j: (l, 0, j)),
        compiler_params=_cparams(("arbitrary", "arbitrary")),
        name="adaln_mod",
    )(c8, ada_w, ada_b.reshape(depth, 1, w))


def _conv_body(h_ref, modr_ref, ng_ref, win_ref, scw_ref, cfw_ref, cfv_ref, wout_ref, o_ref,
               gbuf, cvbuf, ycat, wbc, *, tm, sc, cf, ka, kb, ha, hb, rc):
    @pl.when((pl.program_id(0) == 0) & (pl.program_id(1) == 0))
    def _():
        for k in range(kb):
            wbc[k] = jnp.broadcast_to(cfw_ref[k:k + 1, :], (SUBLANES, cf))

    @pl.when(pl.program_id(1) == 0)
    def _():
        gbuf[0, 0:hb, :] = jnp.zeros((hb, cf), F32)
        cvbuf[0:ha, :] = jnp.zeros((ha, sc), F32)

    x = h_ref[...]
    m = modr_ref[0]
    u16 = _norm_mod(x, ng_ref[...], m[0:1], m[1:2]).astype(BF16)

    zb = jnp.dot(u16, win_ref[:, 3 * sc:3 * sc + 2 * cf], preferred_element_type=F32)
    gbuf[0, hb:hb + tm, :] = zb[:, 0:cf] * jax.nn.sigmoid(zb[:, cf:2 * cf])
    span = tm + hb - SUBLANES
    for b in range(1, SUBLANES):
        for c0 in range(0, span, 64):
            c1 = min(c0 + 64, span)
            gbuf[b, c0:c1, :] = gbuf[0, b + c0:b + c1, :]

    za = jnp.dot(u16, win_ref[:, 0:3 * sc], preferred_element_type=F32)
    cv = za[:, sc:2 * sc] * za[:, 2 * sc:3 * sc]
    cvbuf[ha:ha + tm, :] = cv
    acc = scw_ref[ka - 1:ka, :] * cv
    for k in range(ka - 1):
        off = ha - (ka - 1) + k
        acc = acc + scw_ref[k:k + 1, :] * cvbuf[off:off + tm, :]
    ycat[:, 0:sc] = (za[:, 0:sc] * acc).astype(BF16)
    y = jnp.dot(ycat[:, 0:sc], wout_ref[0:sc, :], preferred_element_type=F32)

    cf_b = cfv_ref[0:1, :]
    ln_g = cfv_ref[1:2, :]
    ln_b = cfv_ref[2:3, :]
    for r0 in range(0, tm, rc):
        rows = range(r0, r0 + rc, SUBLANES)
        w_last = wbc[kb - 1]
        accs = [w_last * gbuf[0, hb + r:hb + r + SUBLANES, :] for r in rows]
        for k in range(kb - 1):
            off = hb - (kb - 1) + k
            al = off // SUBLANES * SUBLANES
            wk = wbc[k]
            for c, r in enumerate(rows):
                accs[c] = accs[c] + wk * gbuf[off % SUBLANES, al + r:al + r + SUBLANES, :]
        a = jnp.concatenate(accs, axis=0) + cf_b
        mu = jnp.mean(a, axis=-1, keepdims=True)
        ac = a - mu
        var = jnp.mean(ac * ac, axis=-1, keepdims=True)
        yn = ac * lax.rsqrt(var + NORM_EPS) * ln_g + ln_b
        ycat[r0:r0 + rc, sc:sc + cf] = (yn * jax.nn.sigmoid(yn)).astype(BF16)

    cvbuf[0:ha, :] = cvbuf[tm:tm + ha, :]
    gbuf[0, 0:hb, :] = gbuf[0, tm:tm + hb, :]

    y = y + jnp.dot(ycat[:, sc:sc + cf], wout_ref[sc:sc + cf, :], preferred_element_type=F32)
    o_ref[...] = x + m[2:3] * y


def _conv_mixer_call(h, modr, ng, w_in, sc_w, cf_w, cf_b, ln_g, ln_b, w_out, bsz, seq):
    n, d = h.shape
    ka, sc = sc_w.shape
    kb, cf = cf_w.shape
    ha = -(-(ka - 1) // SUBLANES) * SUBLANES
    hb = -(-(kb - 1) // SUBLANES) * SUBLANES
    tm = min(ROW_TILE, seq)
    assert seq % tm == 0 and tm >= hb and tm >= ha
    ns = seq // tm
    scw = jnp.zeros((SUBLANES, sc), F32).at[:ka].set(sc_w)
    cfw = jnp.zeros((-(-kb // SUBLANES) * SUBLANES, cf), F32).at[:kb].set(cf_w)
    cfv = jnp.zeros((SUBLANES, cf), F32).at[0].set(cf_b).at[1].set(ln_g).at[2].set(ln_b)
    body = functools.partial(_conv_body, tm=tm, sc=sc, cf=cf, ka=ka, kb=kb, ha=ha, hb=hb,
                             rc=min(16, tm))
    return pl.pallas_call(
        body,
        out_shape=jax.ShapeDtypeStruct((n, d), F32),
        grid=(bsz, ns),
        in_specs=[
            pl.BlockSpec((tm, d), lambda b, s: (b * ns + s, 0)),
            pl.BlockSpec((1, SUBLANES, d), lambda b, s: (b, 0, 0)),
            _resident((1, d)),
            _resident(w_in.shape),
            _resident(scw.shape),
            _resident(cfw.shape),
            _resident(cfv.shape),
            _resident(w_out.shape),
        ],
        out_specs=pl.BlockSpec((tm, d), lambda b, s: (b * ns + s, 0)),
        scratch_shapes=[
            pltpu.VMEM((SUBLANES, tm + hb, cf), F32),
            pltpu.VMEM((tm + ha, sc), F32),
            pltpu.VMEM((tm, sc + cf), BF16),
            pltpu.VMEM((cfw.shape[0], SUBLANES, cf), F32),
        ],
        compiler_params=_cparams(("arbitrary", "arbitrary")),
        name="conv_mixer",
    )(h, modr, ng, w_in.astype(BF16), scw, cfw, cfv, w_out.astype(BF16))


def _norm_body(h_ref, modr_ref, ng_ref, o_ref):
    m = modr_ref[0]
    o_ref[...] = _norm_mod(h_ref[...], ng_ref[...], m[0:1], m[1:2]).astype(BF16)


def _norm_call(h, modr, ng, seq):
    n, d = h.shape
    tm = min(2 * ROW_TILE, seq)
    tpb = seq // tm
    return pl.pallas_call(
        _norm_body,
        out_shape=jax.ShapeDtypeStruct((n, d), BF16),
        grid=(n // tm,),
        in_specs=[
            pl.BlockSpec((tm, d), lambda i: (i, 0)),
            pl.BlockSpec((1, SUBLANES, d), lambda i: (i // tpb, 0, 0)),
            _resident((1, d)),
        ],
        out_specs=pl.BlockSpec((tm, d), lambda i: (i, 0)),
        compiler_params=_cparams(("arbitrary",)),
        name="s5_norm",
    )(h, modr, ng)


def _s5_prep(lam_re, lam_im, log_dt, b_re, b_im, c_re, c_im, chunk):
    hi = lax.Precision.HIGHEST
    dt = jnp.exp(log_dt)[:, None]
    mag = jnp.exp(lam_re * dt)
    lb_re = mag * jnp.cos(lam_im * dt)
    lb_im = mag * jnp.sin(lam_im * dt)
    den = lam_re * lam_re + lam_im * lam_im
    f_re = ((lb_re - 1) * lam_re + lb_im * lam_im) / den
    f_im = (lb_im * lam_re - (lb_re - 1) * lam_im) / den
    bb_re = f_re[..., None] * b_re - f_im[..., None] * b_im
    bb_im = f_re[..., None] * b_im + f_im[..., None] * b_re
    g, p, hh = bb_re.shape
    prs, pis = [jnp.ones_like(lb_re)], [jnp.zeros_like(lb_im)]
    for _ in range(chunk):
        pr, pi = prs[-1], pis[-1]
        prs.append(pr * lb_re - pi * lb_im)
        pis.append(pr * lb_im + pi * lb_re)
    pr = jnp.stack(prs)
    pi = jnp.stack(pis)
    kmat = (jnp.einsum('ghp,tgp,gpk->tghk', c_re, pr[:chunk], bb_re, precision=hi)
            - jnp.einsum('ghp,tgp,gpk->tghk', c_re, pi[:chunk], bb_im, precision=hi)
            - jnp.einsum('ghp,tgp,gpk->tghk', c_im, pr[:chunk], bb_im, precision=hi)
            - jnp.einsum('ghp,tgp,gpk->tghk', c_im, pi[:chunk], bb_re, precision=hi))
    s_idx = jnp.arange(chunk)[:, None]
    t_idx = jnp.arange(chunk)[None, :]
    lag = jnp.maximum(t_idx - s_idx, 0)
    kt = jnp.where((t_idx >= s_idx)[:, :, None, None, None], kmat[lag], 0.0)
    tmat = kt.transpose(2, 0, 4, 1, 3).reshape(g, chunk * hh, chunk * hh)
    prr = pr[chunk - 1::-1][:chunk]
    pir = pi[chunk - 1::-1][:chunk]
    w_re = prr[..., None] * bb_re[None] - pir[..., None] * bb_im[None]
    w_im = prr[..., None] * bb_im[None] + pir[..., None] * bb_re[None]
    w_in = jnp.concatenate([w_re.transpose(1, 0, 3, 2), w_im.transpose(1, 0, 3, 2)], axis=-1)
    w_in = w_in.reshape(g, chunk * hh, 2 * p)
    p1r = pr[1:chunk + 1][:, :, None, :]
    p1i = pi[1:chunk + 1][:, :, None, :]
    co_r = c_re[None] * p1r - c_im[None] * p1i
    co_i = -c_re[None] * p1i - c_im[None] * p1r
    w_out = jnp.concatenate([co_r.transpose(1, 3, 0, 2), co_i.transpose(1, 3, 0, 2)], axis=1)
    w_out = w_out.reshape(g, 2 * p, chunk * hh)
    return tmat, w_in, w_out, pr[chunk], pi[chunk]


def _s5_lane_blocks(tmat, w_in, w_out, chunk, hh, p):
    g = tmat.shape[0]
    gpb = LANES // hh
    nlb = g // gpb
    wide = chunk * LANES
    eye = jnp.eye(gpb, dtype=BF16)
    t6 = tmat.astype(BF16).reshape(nlb, gpb, chunk, hh, chunk, hh)
    t_blk = jnp.einsum('jaskth,ab->jsaktbh', t6, eye).reshape(nlb, wide, wide)
    wi = w_in.astype(BF16).reshape(nlb, gpb, chunk, hh, 2, p)
    w_in_blk = jnp.einsum('jaskqp,ab->jsakqbp', wi, eye).reshape(nlb, wide, 2 * gpb * p)
    wo = w_out.astype(BF16).reshape(nlb, gpb, 2, p, chunk, hh)
    w_out_blk = jnp.einsum('jaqpth,ab->jqaptbh', wo, eye).reshape(nlb, 2 * gpb * p, wide)
    return t_blk, w_in_blk, w_out_blk


def _fold_chunks(u_ref, ubuf, ucat, chunk):
    ubuf[...] = u_ref[...].astype(F32)
    for s in range(chunk):
        ucat[:, s * LANES:(s + 1) * LANES] = ubuf[:, s, :].astype(BF16)


def _s5_state_body(u_ref, w_ref, ore_ref, oim_ref, ubuf, ucat, *, chunk, half):
    _fold_chunks(u_ref, ubuf, ucat, chunk)
    s = jnp.dot(ucat[...], w_ref[0], preferred_element_type=F32)
    ore_ref[...] = s[:, 0:half]
    oim_ref[...] = s[:, half:2 * half]


def _s5_state_call(u3, w_in_blk, gp):
    nc, chunk, d = u3.shape
    nlb = d // LANES
    half = w_in_blk.shape[-1] // 2
    cb = _largest_divisor(nc, (256, 128, 64, 32, 16, 8))
    return pl.pallas_call(
        functools.partial(_s5_state_body, chunk=chunk, half=half),
        out_shape=(jax.ShapeDtypeStruct((nc, gp), F32), jax.ShapeDtypeStruct((nc, gp), F32)),
        grid=(nlb, nc // cb),
        in_specs=[
            pl.BlockSpec((cb, chunk, LANES), lambda j, i: (i, 0, j)),
            pl.BlockSpec((1, chunk * LANES, 2 * half), lambda j, i: (j, 0, 0)),
        ],
        out_specs=(pl.BlockSpec((cb, half), lambda j, i: (i, j)),
                   pl.BlockSpec((cb, half), lambda j, i: (i, j))),
        scratch_shapes=[pltpu.VMEM((cb, chunk, LANES), F32), pltpu.VMEM((cb, chunk * LANES), BF16)],
        compiler_params=_cparams(("arbitrary", "arbitrary")),
        name="s5_chunk_state",
    )(u3, w_in_blk)


def _s5_scan_body(s_ref, a_ref, o_ref, carry, *, cb):
    @pl.when(pl.program_id(1) == 0)
    def _():
        carry[...] = jnp.zeros_like(carry)

    ar = a_ref[0]
    ai = a_ref[1]

    def step(c, st):
        xr, xi = st
        o_ref[0, c] = xr
        o_ref[1, c] = xi
        return (ar * xr - ai * xi + s_ref[0, c], ar * xi + ai * xr + s_ref[1, c])

    xr, xi = lax.fori_loop(0, cb, step, (carry[0], carry[1]))
    carry[0] = xr
    carry[1] = xi


def _s5_scan_call(s4, a2, bsz):
    _, nc, r, _ = s4.shape
    ncb = nc // bsz
    cb = _largest_divisor(ncb, (64, 32, 16, 8, 4, 2, 1))
    nb = ncb // cb
    return pl.pallas_call(
        functools.partial(_s5_scan_body, cb=cb),
        out_shape=jax.ShapeDtypeStruct(s4.shape, F32),
        grid=(bsz, nb),
        in_specs=[
            pl.BlockSpec((2, cb, r, LANES), lambda b, j: (0, b * nb + j, 0, 0)),
            _resident(a2.shape),
        ],
        out_specs=pl.BlockSpec((2, cb, r, LANES), lambda b, j: (0, b * nb + j, 0, 0)),
        scratch_shapes=[pltpu.VMEM((2, r, LANES), F32)],
        compiler_params=_cparams(("arbitrary", "arbitrary")),
        name="s5_state_scan",
    )(s4, a2)


def _s5_out_body(u_ref, t_ref, xre_ref, xim_ref, w_ref, o_ref, ubuf, ucat, ybuf, *, chunk, half):
    _fold_chunks(u_ref, ubuf, ucat, chunk)
    y = jnp.dot(ucat[...], t_ref[0], preferred_element_type=F32)
    y = y + jnp.dot(xre_ref[...], w_ref[0, 0:half, :], preferred_element_type=F32)
    y = y + jnp.dot(xim_ref[...], w_ref[0, half:2 * half, :], preferred_element_type=F32)
    for t in range(chunk):
        ybuf[:, t, :] = y[:, t * LANES:(t + 1) * LANES]
    o_ref[...] = ybuf[...].astype(BF16)


def _s5_out_call(u3, t_blk, xs_re, xs_im, w_out_blk):
    nc, chunk, d = u3.shape
    nlb = d // LANES
    half = w_out_blk.shape[1] // 2
    cb = _largest_divisor(nc, (256, 128, 64, 32, 16, 8))
    return pl.pallas_call(
        functools.partial(_s5_out_body, chunk=chunk, half=half),
        out_shape=jax.ShapeDtypeStruct((nc, chunk, d), BF16),
        grid=(nlb, nc // cb),
        in_specs=[
            pl.BlockSpec((cb, chunk, LANES), lambda j, i: (i, 0, j)),
            pl.BlockSpec((1, chunk * LANES, chunk * LANES), lambda j, i: (j, 0, 0)),
            pl.BlockSpec((cb, half), lambda j, i: (i, j)),
            pl.BlockSpec((cb, half), lambda j, i: (i, j)),
            pl.BlockSpec((1, 2 * half, chunk * LANES), lambda j, i: (j, 0, 0)),
        ],
        out_specs=pl.BlockSpec((cb, chunk, LANES), lambda j, i: (i, 0, j)),
        scratch_shapes=[pltpu.VMEM((cb, chunk, LANES), F32), pltpu.VMEM((cb, chunk * LANES), BF16),
                        pltpu.VMEM((cb, chunk, LANES), F32)],
        compiler_params=_cparams(("arbitrary", "arbitrary")),
        name="s5_chunk_out",
    )(u3, t_blk, xs_re, xs_im, w_out_blk)


def _s5_post_body(h_ref, y_ref, modr_ref, ng_ref, dsk_ref, wglu_ref, o_ref, *, d):
    x = h_ref[...]
    m = modr_ref[0]
    u = _norm_mod(x, ng_ref[...], m[0:1], m[1:2])
    yy = jax.nn.gelu(y_ref[...].astype(F32) + dsk_ref[...] * u)
    r = jnp.dot(yy.astype(BF16), wglu_ref[...], preferred_element_type=F32)
    o_ref[...] = x + m[2:3] * (r[:, 0:d] * jax.nn.sigmoid(r[:, d:2 * d]))


def _s5_post_call(h, y, modr, ng, d_skip, w_glu, seq):
    n, d = h.shape
    tm = min(ROW_TILE, seq)
    tpb = seq // tm
    return pl.pallas_call(
        functools.partial(_s5_post_body, d=d),
        out_shape=jax.ShapeDtypeStruct((n, d), F32),
        grid=(n // tm,),
        in_specs=[
            pl.BlockSpec((tm, d), lambda i: (i, 0)),
            pl.BlockSpec((tm, d), lambda i: (i, 0)),
            pl.BlockSpec((1, SUBLANES, d), lambda i: (i // tpb, 0, 0)),
            _resident((1, d)),
            _resident((1, d)),
            _resident(w_glu.shape),
        ],
        out_specs=pl.BlockSpec((tm, d), lambda i: (i, 0)),
        compiler_params=_cparams(("arbitrary",)),
        name="s5_glu",
    )(h, y, modr, ng, d_skip, w_glu.astype(BF16))


def _s5_mixer(h, modr, ng, lam_re, lam_im, log_dt, b_re, b_im, c_re, c_im, d_skip, w_glu,
              bsz, seq):
    n, d = h.shape
    g, p, hh = b_re.shape
    chunk = min(S5_CHUNK, seq)
    nc = n // chunk
    tmat, w_in, w_out, a_re, a_im = _s5_prep(lam_re, lam_im, log_dt, b_re, b_im, c_re, c_im, chunk)
    t_blk, w_in_blk, w_out_blk = _s5_lane_blocks(tmat, w_in, w_out, chunk, hh, p)
    u3 = _norm_call(h, modr, ng, seq).reshape(nc, chunk, d)
    s_re, s_im = _s5_state_call(u3, w_in_blk, g * p)
    r = g * p // LANES
    s4 = jnp.stack([s_re, s_im]).reshape(2, nc, r, LANES)
    a2 = jnp.stack([a_re, a_im]).reshape(2, r, LANES)
    xs4 = _s5_scan_call(s4, a2, bsz)
    xs_re = xs4[0].reshape(nc, g * p).astype(BF16)
    xs_im = xs4[1].reshape(nc, g * p).astype(BF16)
    y = _s5_out_call(u3, t_blk, xs_re, xs_im, w_out_blk).reshape(n, d)
    return _s5_post_call(h, y, modr, ng, d_skip.reshape(1, d), w_glu, seq)


def _router_body(h_ref, modr_ref, ng_ref, whi_ref, wlo_ref, rb_ref, u_ref, meta_ref, cnt_ref,
                 carry, *, n_exp, tm):
    @pl.when(pl.program_id(0) == 0)
    def _():
        carry[...] = jnp.zeros_like(carry)

    m = modr_ref[0]
    u = _norm_mod(h_ref[...], ng_ref[...], m[3:4], m[4:5])
    u_ref[...] = u
    uh = u.astype(BF16)
    ul = (u - uh.astype(F32)).astype(BF16)
    whi = whi_ref[...]
    logits = (jnp.dot(uh, whi, preferred_element_type=F32)
              + jnp.dot(ul, whi, preferred_element_type=F32)
              + jnp.dot(uh, wlo_ref[...], preferred_element_type=F32)) + rb_ref[...]
    lane = lax.broadcasted_iota(jnp.int32, (tm, LANES), 1)
    lane_f = lane.astype(F32)
    l = jnp.where(lane < n_exp, logits, NEG_BIG)
    tops, idxs, sels = [], [], []
    for _ in range(TOP_K):
        mk = jnp.max(l, axis=-1, keepdims=True)
        ik = jnp.min(jnp.where(l == mk, lane_f, 1.0e9), axis=-1, keepdims=True)
        sk = lane_f == ik
        l = jnp.where(sk, NEG_BIG, l)
        tops.append(mk)
        idxs.append(ik)
        sels.append(sk)
    es = [jnp.exp(t - tops[0]) for t in tops]
    den = es[0]
    for e in es[1:]:
        den = den + e
    chosen = jnp.zeros((tm, LANES), F32)
    for sk in sels:
        chosen = jnp.where(sk, 1.0, chosen)
    row = lax.broadcasted_iota(jnp.int32, (tm, tm), 0)
    col = lax.broadcasted_iota(jnp.int32, (tm, tm), 1)
    tri = jnp.where(row > col, 1.0, 0.0).astype(BF16)
    rank_mat = jnp.dot(tri, chosen.astype(BF16), preferred_element_type=F32) + carry[0:1, :]
    meta = jnp.zeros((tm, LANES), F32)
    for k in range(TOP_K):
        rk = jnp.sum(jnp.where(sels[k], rank_mat, 0.0), axis=-1, keepdims=True)
        meta = jnp.where(lane == k, idxs[k], meta)
        meta = jnp.where(lane == TOP_K + k, rk, meta)
        meta = jnp.where(lane == 2 * TOP_K + k, es[k] / den, meta)
    meta_ref[...] = meta
    carry[...] = carry[...] + jnp.sum(chosen, axis=0, keepdims=True)
    cnt_ref[...] = carry[...]


def _router_call(h, modr, ng, w_router, b_router, seq):
    n, d = h.shape
    n_exp = w_router.shape[1]
    tm = min(ROW_TILE, seq)
    tpb = seq // tm
    wpad = jnp.zeros((d, LANES), F32).at[:, :n_exp].set(w_router)
    whi = wpad.astype(BF16)
    wlo = (wpad - whi.astype(F32)).astype(BF16)
    rb = jnp.zeros((1, LANES), F32).at[0, :n_exp].set(b_router)
    return pl.pallas_call(
        functools.partial(_router_body, n_exp=n_exp, tm=tm),
        out_shape=(
            jax.ShapeDtypeStruct((n, d), F32),
            jax.ShapeDtypeStruct((n, LANES), F32),
            jax.ShapeDtypeStruct((SUBLANES, LANES), F32),
        ),
        grid=(n // tm,),
        in_specs=[
            pl.BlockSpec((tm, d), lambda i: (i, 0)),
            pl.BlockSpec((1, SUBLANES, d), lambda i: (i // tpb, 0, 0)),
            _resident((1, d)),
            _resident(whi.shape),
            _resident(wlo.shape),
            _resident(rb.shape),
        ],
        out_specs=(
            pl.BlockSpec((tm, d), lambda i: (i, 0)),
            pl.BlockSpec((tm, LANES), lambda i: (i, 0)),
            pl.BlockSpec((SUBLANES, LANES), lambda i: (0, 0)),
        ),
        scratch_shapes=[pltpu.VMEM((SUBLANES, LANES), F32)],
        compiler_params=_cparams(("arbitrary",)),
        name="moe_router",
    )(h, modr, ng, whi, wlo, rb)


def _row_copy(src_ref, src_row, dst_ref, dst_row, sem):
    return pltpu.make_async_copy(src_ref.at[pl.ds(src_row, 1)], dst_ref.at[pl.ds(dst_row, 1)], sem)


def _rows_wait(hbm_ref, rows, sem):
    pltpu.make_async_copy(hbm_ref.at[pl.ds(0, rows)], hbm_ref.at[pl.ds(0, rows)], sem).wait()


def _dispatch_body(npad_ref, padrow_ref, dest_ref, u_ref, xs_ref, zrow, sem, *, td, pp):
    i = pl.program_id(0)
    zrow[...] = jnp.zeros_like(zrow)
    n_pad = jnp.clip(npad_ref[0] - i * pp, 0, pp)

    def issue(r, c):
        for k in range(TOP_K):
            _row_copy(u_ref, r, xs_ref, dest_ref[TOP_K * r + k], sem).start()
        return c

    lax.fori_loop(0, td, issue, 0, unroll=DMA_UNROLL)

    def issue_pad(j, c):
        _row_copy(zrow, 0, xs_ref, padrow_ref[i * pp + j], sem).start()
        return c

    lax.fori_loop(0, n_pad, issue_pad, 0)

    _rows_wait(xs_ref, TOP_K * td, sem)

    def wait_one(j, c):
        _row_copy(zrow, 0, xs_ref, 0, sem).wait()
        return c

    lax.fori_loop(0, n_pad, wait_one, 0)


def _dispatch_call(u, dest, pad_rows, n_pad, n_slots):
    n, d = u.shape
    td = min(ROW_TILE, n)
    steps = n // td
    pp = -(-pad_rows.shape[0] // steps)
    return pl.pallas_call(
        functools.partial(_dispatch_body, td=td, pp=pp),
        out_shape=jax.ShapeDtypeStruct((n_slots, d), F32),
        grid_spec=pltpu.PrefetchScalarGridSpec(
            num_scalar_prefetch=2,
            grid=(steps,),
            in_specs=[
                pl.BlockSpec((TOP_K * td,), lambda i, npad, prow: (i,), memory_space=pltpu.SMEM),
                pl.BlockSpec((td, d), lambda i, npad, prow: (i, 0)),
            ],
            out_specs=pl.BlockSpec(memory_space=pl.ANY),
            scratch_shapes=[pltpu.VMEM((SUBLANES, d), F32), pltpu.SemaphoreType.DMA(())],
        ),
        compiler_params=pltpu.CompilerParams(dimension_semantics=("arbitrary",),
                                             vmem_limit_bytes=VMEM_LIMIT, has_side_effects=True,
                                             disable_bounds_checks=True),
        name="moe_dispatch",
    )(n_pad, pad_rows, dest, u)


def _expert_body(bexp_ref, nused_ref, x_ref, wg_ref, bg_ref, wu_ref, bu_ref, wd_ref, bd_ref, o_ref):
    @pl.when(pl.program_id(0) < nused_ref[0])
    def _():
        x = x_ref[...].astype(BF16)
        g = jnp.dot(x, wg_ref[0, 0], preferred_element_type=F32) + bg_ref[0, 0]
        g = jnp.minimum(g, SWIGLU_LIMIT)
        v = jnp.dot(x, wu_ref[0, 0], preferred_element_type=F32) + bu_ref[0, 0]
        v = jnp.clip(v, -SWIGLU_LIMIT, SWIGLU_LIMIT)
        a = g * jax.nn.sigmoid(SWIGLU_ALPHA * g) * (v + 1.0)
        o_ref[...] = jnp.dot(a.astype(BF16), wd_ref[0, 0], preferred_element_type=F32) + bd_ref[0, 0]


def _expert_call(xs, block_exp, n_used, layer, wg, bg, wu, bu, wd, bd):
    n_slots, d = xs.shape
    depth, n_exp, _, f = wg.shape
    te = EXPERT_TILE
    nblk = n_slots // te

    def xmap(b, be, nu):
        return (jnp.minimum(b, nu[0] - 1), 0)

    def wmap(b, be, nu):
        return (layer, be[b], 0, 0)

    return pl.pallas_call(
        _expert_body,
        out_shape=jax.ShapeDtypeStruct((n_slots, d), F32),
        grid_spec=pltpu.PrefetchScalarGridSpec(
            num_scalar_prefetch=2,
            grid=(nblk,),
            in_specs=[
                pl.BlockSpec((te, d), xmap),
                pl.BlockSpec((1, 1, d, f), wmap),
                pl.BlockSpec((1, 1, 1, f), wmap),
                pl.BlockSpec((1, 1, d, f), wmap),
                pl.BlockSpec((1, 1, 1, f), wmap),
                pl.BlockSpec((1, 1, f, d), wmap),
                pl.BlockSpec((1, 1, 1, d), wmap),
            ],
            out_specs=pl.BlockSpec((te, d), xmap),
        ),
        compiler_params=_cparams(("arbitrary",)),
        name="moe_experts",
    )(block_exp, n_used, xs, wg, bg.reshape(depth, n_exp, 1, f), wu, bu.reshape(depth, n_exp, 1, f),
      wd, bd.reshape(depth, n_exp, 1, d))


def _combine_body(dest_ref, dnext_ref, h_ref, meta_ref, modr_ref, fg_ref, ys_ref, o_ref, buf, sem,
                  *, tc, final):
    i = pl.program_id(0)
    slot = i % 2

    def gather(d_ref, s):
        def issue(r, c):
            for k in range(TOP_K):
                pltpu.make_async_copy(ys_ref.at[pl.ds(d_ref[TOP_K * r + k], 1)],
                                      buf.at[s, k, pl.ds(r, 1)], sem.at[s]).start()
            return c

        lax.fori_loop(0, tc, issue, 0, unroll=DMA_UNROLL)

    @pl.when(i == 0)
    def _():
        gather(dest_ref, 0)

    @pl.when(i + 1 < pl.num_programs(0))
    def _():
        gather(dnext_ref, 1 - slot)

    _rows_wait(ys_ref, TOP_K * tc, sem.at[slot])

    meta = meta_ref[...]
    acc = meta[:, 2 * TOP_K:2 * TOP_K + 1] * buf[slot, 0]
    for k in range(1, TOP_K):
        acc = acc + meta[:, 2 * TOP_K + k:2 * TOP_K + k + 1] * buf[slot, k]
    out = h_ref[...] + modr_ref[0][5:6] * acc
    if final:
        ms = jnp.mean(out * out, axis=-1, keepdims=True)
        out = out * lax.rsqrt(ms + NORM_EPS) * fg_ref[...]
    o_ref[...] = out


def _combine_call(h, meta, modr, final_g, ys, dest, seq, final):
    n, d = h.shape
    tc = min(ROW_TILE // 2, seq)
    tpb = seq // tc
    steps = n // tc
    return pl.pallas_call(
        functools.partial(_combine_body, tc=tc, final=final),
        out_shape=jax.ShapeDtypeStruct((n, d), F32),
        grid=(steps,),
        in_specs=[
            pl.BlockSpec((TOP_K * tc,), lambda i: (i,), memory_space=pltpu.SMEM),
            pl.BlockSpec((TOP_K * tc,), lambda i: (jnp.minimum(i + 1, steps - 1),),
                         memory_space=pltpu.SMEM),
            pl.BlockSpec((tc, d), lambda i: (i, 0)),
            pl.BlockSpec((tc, LANES), lambda i: (i, 0)),
            pl.BlockSpec((1, SUBLANES, d), lambda i: (i // tpb, 0, 0)),
            _resident((1, d)),
            pl.BlockSpec(memory_space=pl.ANY),
        ],
        out_specs=pl.BlockSpec((tc, d), lambda i: (i, 0)),
        scratch_shapes=[pltpu.VMEM((2, TOP_K, tc, d), F32), pltpu.SemaphoreType.DMA((2,))],
        compiler_params=pltpu.CompilerParams(dimension_semantics=("arbitrary",),
                                             vmem_limit_bytes=VMEM_LIMIT,
                                             disable_bounds_checks=True),
        name="moe_combine",
    )(dest, dest, h, meta, modr, final_g, ys)


def _moe(h, modr, ng, w_router, b_router, layer, wg, bg, wu, bu, wd, bd, final_g, seq, final):
    n, d = h.shape
    n_exp = w_router.shape[1]
    te = EXPERT_TILE
    n_slots = n * TOP_K + n_exp * te
    nblk = n_slots // te
    u, meta, cnt = _router_call(h, modr, ng, w_router, b_router, seq)

    idx = meta[:, 0:TOP_K].astype(jnp.int32)
    rank = meta[:, TOP_K:2 * TOP_K].astype(jnp.int32)
    counts = cnt[0, :n_exp].astype(jnp.int32)
    padded = (counts + te - 1) // te * te
    pad_ends = jnp.cumsum(padded)
    pad_starts = pad_ends - padded
    dest = (pad_starts[idx] + rank).reshape(-1)
    n_used = pad_ends[-1] // te
    blk = jnp.minimum(jnp.arange(nblk, dtype=jnp.int32), n_used - 1)
    block_exp = jnp.minimum(jnp.sum(pad_ends[None, :] <= (blk * te)[:, None], axis=1),
                            n_exp - 1).astype(jnp.int32)
    within = jnp.arange(te, dtype=jnp.int32)[None, :]
    n_fill = (padded - counts)[:, None]
    fill_rows = jnp.where(within < n_fill, (pad_starts + counts)[:, None] + within, n_slots)
    pad_rows = jnp.sort(fill_rows.reshape(-1)).astype(jnp.int32)
    n_pad = jnp.sum(padded - counts).astype(jnp.int32).reshape(1)

    xs = _dispatch_call(u, dest, pad_rows, n_pad, n_slots)
    ys = _expert_call(xs, block_exp, n_used.astype(jnp.int32).reshape(1), layer,
                      wg, bg, wu, bu, wd, bd)
    return _combine_call(h, meta, modr, final_g, ys, dest, seq, final)


def kernel(x, c, ada_w, ada_b, norm_mix_g, norm_ffn_g, cv_w_in, cv_sc_w, cv_cf_w, cv_cf_b, cv_ln_g, cv_ln_b, cv_w_out, s5_lam_re, s5_lam_im, s5_log_dt, s5_b_re, s5_b_im, s5_c_re, s5_c_im, s5_d, s5_w_glu, router_w, router_b, exp_w_gate, exp_b_gate, exp_w_up, exp_b_up, exp_w_down, exp_b_down, final_g):
    bsz, seq, d = x.shape
    depth = ada_w.shape[0]
    n = bsz * seq
    assert bsz <= SUBLANES
    h = x.reshape(n, d)
    mod = _mod_call(c, ada_w, ada_b)
    fg = final_g.reshape(1, d)
    wg16 = exp_w_gate.astype(BF16)
    wu16 = exp_w_up.astype(BF16)
    wd16 = exp_w_down.astype(BF16)
    for layer in range(depth):
        modr = jnp.zeros((bsz, SUBLANES, d), F32).at[:, :6].set(mod[layer, :bsz].reshape(bsz, 6, d))
        ng = norm_mix_g[layer].reshape(1, d)
        j = layer // 2
        if layer % 2 == 0:
            h = _conv_mixer_call(h, modr, ng, cv_w_in[j], cv_sc_w[j], cv_cf_w[j], cv_cf_b[j],
                                 cv_ln_g[j], cv_ln_b[j], cv_w_out[j], bsz, seq)
        else:
            h = _s5_mixer(h, modr, ng, s5_lam_re[j], s5_lam_im[j], s5_log_dt[j], s5_b_re[j],
                          s5_b_im[j], s5_c_re[j], s5_c_im[j], s5_d[j], s5_w_glu[j], bsz, seq)
        h = _moe(h, modr, norm_ffn_g[layer].reshape(1, d), router_w[layer], router_b[layer],
                 layer, wg16, exp_b_gate, wu16, exp_b_up, wd16, exp_b_down, fg, seq,
                 layer == depth - 1)
    return h.reshape(bsz, seq, d)
```

```python
import functools

import jax
import jax.numpy as jnp
from jax import lax
from jax.experimental import pallas as pl
from jax.experimental.pallas import tpu as pltpu

F32 = jnp.float32
BF16 = jnp.bfloat16
NORM_EPS = 1e-5
TOP_K = 4
SWIGLU_LIMIT = 7.0
SWIGLU_ALPHA = 1.702
LANES = 128
SUBLANES = 8
VMEM_LIMIT = 56 * 1024 * 1024
S5_CHUNK = 16
ROW_TILE = 256
EXPERT_TILE = 256
DMA_UNROLL = 8
NEG_BIG = -3.0e38


def _cparams(sem):
    return pltpu.CompilerParams(dimension_semantics=sem, vmem_limit_bytes=VMEM_LIMIT)


def _largest_divisor(n, candidates):
    for c in candidates:
        if n % c == 0:
            return c
    return n


def _norm_mod(x, g, shift, scale):
    ms = jnp.mean(x * x, axis=-1, keepdims=True)
    return x * lax.rsqrt(ms + NORM_EPS) * g * (1.0 + scale) + shift


def _resident(shape):
    nd = len(shape)
    return pl.BlockSpec(shape, lambda *_: (0,) * nd, pipeline_mode=pl.Buffered(1))


def _mod_body(c_ref, w_ref, b_ref, o_ref):
    c = c_ref[...]
    ca = (c * jax.nn.sigmoid(c)).astype(BF16)
    o_ref[0] = jnp.dot(ca, w_ref[0].astype(BF16), preferred_element_type=F32) + b_ref[0]


def _mod_call(c, ada_w, ada_b):
    depth, d, w = ada_w.shape
    b = c.shape[0]
    c8 = jnp.zeros((SUBLANES, d), F32).at[:b].set(c)
    tn = _largest_divisor(w, (1536, 1024, 768, 512, 384, 256, 128))
    return pl.pallas_call(
        _mod_body,
        out_shape=jax.ShapeDtypeStruct((depth, SUBLANES, w), F32),
        grid=(depth, w // tn),
        in_specs=[
            pl.BlockSpec((SUBLANES, d), lambda l, j: (0, 0)),
            pl.BlockSpec((1, d, tn), lambda l, j: (l, 0, j)),
            pl.BlockSpec((1, 1, tn), lambda l, j: (l, 0, j)),
        ],
        out_specs=pl.BlockSpec((1, SUBLANES, tn), lambda l, j: (l, 0, j)),
        compiler_params=_cparams(("arbitrary", "arbitrary")),
        name="adaln_mod",
    )(c8, ada_w, ada_b.reshape(depth, 1, w))


def _conv_body(h_ref, modr_ref, ng_ref, win_ref, scw_ref, cfw_ref, cfv_ref, wout_ref, o_ref,
               gbuf, cvbuf, ycat, wbc, *, tm, sc, cf, ka, kb, ha, hb, rc):
    @pl.when((pl.program_id(0) == 0) & (pl.program_id(1) == 0))
    def _():
        for k in range(kb):
            wbc[k] = jnp.broadcast_to(cfw_ref[k:k + 1, :], (SUBLANES, cf))

    @pl.when(pl.program_id(1) == 0)
    def _():
        gbuf[0, 0:hb, :] = jnp.zeros((hb, cf), F32)
        cvbuf[0:ha, :] = jnp.zeros((ha, sc), F32)

    x = h_ref[...]
    m = modr_ref[0]
    u16 = _norm_mod(x, ng_ref[...], m[0:1], m[1:2]).astype(BF16)

    zb = jnp.dot(u16, win_ref[:, 3 * sc:3 * sc + 2 * cf], preferred_element_type=F32)
    gbuf[0, hb:hb + tm, :] = zb[:, 0:cf] * jax.nn.sigmoid(zb[:, cf:2 * cf])
    span = tm + hb - SUBLANES
    for b in range(1, SUBLANES):
        for c0 in range(0, span, 64):
            c1 = min(c0 + 64, span)
            gbuf[b, c0:c1, :] = gbuf[0, b + c0:b + c1, :]

    za = jnp.dot(u16, win_ref[:, 0:3 * sc], preferred_element_type=F32)
    cv = za[:, sc:2 * sc] * za[:, 2 * sc:3 * sc]
    cvbuf[ha:ha + tm, :] = cv
    acc = scw_ref[ka - 1:ka, :] * cv
    for k in range(ka - 1):
        off = ha - (ka - 1) + k
        acc = acc + scw_ref[k:k + 1, :] * cvbuf[off:off + tm, :]
    ycat[:, 0:sc] = (za[:, 0:sc] * acc).astype(BF16)
    y = jnp.dot(ycat[:, 0:sc], wout_ref[0:sc, :], preferred_element_type=F32)

    cf_b = cfv_ref[0:1, :]
    ln_g = cfv_ref[1:2, :]
    ln_b = cfv_ref[2:3, :]
    for r0 in range(0, tm, rc):
        rows = range(r0, r0 + rc, SUBLANES)
        w_last = wbc[kb - 1]
        accs = [w_last * gbuf[0, hb + r:hb + r + SUBLANES, :] for r in rows]
        for k in range(kb - 1):
            off = hb - (kb - 1) + k
            al = off // SUBLANES * SUBLANES
            wk = wbc[k]
            for c, r in enumerate(rows):
                accs[c] = accs[c] + wk * gbuf[off % SUBLANES, al + r:al + r + SUBLANES, :]
        a = jnp.concatenate(accs, axis=0) + cf_b
        mu = jnp.mean(a, axis=-1, keepdims=True)
        ac = a - mu
        var = jnp.mean(ac * ac, axis=-1, keepdims=True)
        yn = ac * lax.rsqrt(var + NORM_EPS) * ln_g + ln_b
        ycat[r0:r0 + rc, sc:sc + cf] = (yn * jax.nn.sigmoid(yn)).astype(BF16)

    cvbuf[0:ha, :] = cvbuf[tm:tm + ha, :]
    gbuf[0, 0:hb, :] = gbuf[0, tm:tm + hb, :]

    y = y + jnp.dot(ycat[:, sc:sc + cf], wout_ref[sc:sc + cf, :], preferred_element_type=F32)
    o_ref[...] = x + m[2:3] * y


def _conv_mixer_call(h, modr, ng, w_in, sc_w, cf_w, cf_b, ln_g, ln_b, w_out, bsz, seq):
    n, d = h.shape
    ka, sc = sc_w.shape
    kb, cf = cf_w.shape
    ha = -(-(ka - 1) // SUBLANES) * SUBLANES
    hb = -(-(kb - 1) // SUBLANES) * SUBLANES
    tm = min(ROW_TILE, seq)
    assert seq % tm == 0 and tm >= hb and tm >= ha
    ns = seq // tm
    scw = jnp.zeros((SUBLANES, sc), F32).at[:ka].set(sc_w)
    cfw = jnp.zeros((-(-kb // SUBLANES) * SUBLANES, cf), F32).at[:kb].set(cf_w)
    cfv = jnp.zeros((SUBLANES, cf), F32).at[0].set(cf_b).at[1].set(ln_g).at[2].set(ln_b)
    body = functools.partial(_conv_body, tm=tm, sc=sc, cf=cf, ka=ka, kb=kb, ha=ha, hb=hb,
                             rc=min(16, tm))
    return pl.pallas_call(
        body,
        out_shape=jax.ShapeDtypeStruct((n, d), F32),
        grid=(bsz, ns),
        in_specs=[
            pl.BlockSpec((tm, d), lambda b, s: (b * ns + s, 0)),
            pl.BlockSpec((1, SUBLANES, d), lambda b, s: (b, 0, 0)),
            _resident((1, d)),
            _resident(w_in.shape),
            _resident(scw.shape),
            _resident(cfw.shape),
            _resident(cfv.shape),
            _resident(w_out.shape),
        ],
        out_specs=pl.BlockSpec((tm, d), lambda b, s: (b * ns + s, 0)),
        scratch_shapes=[
            pltpu.VMEM((SUBLANES, tm + hb, cf), F32),
            pltpu.VMEM((tm + ha, sc), F32),
            pltpu.VMEM((tm, sc + cf), BF16),
            pltpu.VMEM((cfw.shape[0], SUBLANES, cf), F32),
        ],
        compiler_params=_cparams(("arbitrary", "arbitrary")),
        name="conv_mixer",
    )(h, modr, ng, w_in.astype(BF16), scw, cfw, cfv, w_out.astype(BF16))


def _norm_body(h_ref, modr_ref, ng_ref, o_ref):
    m = modr_ref[0]
    o_ref[...] = _norm_mod(h_ref[...], ng_ref[...], m[0:1], m[1:2]).astype(BF16)


def _norm_call(h, modr, ng, seq):
    n, d = h.shape
    tm = min(2 * ROW_TILE, seq)
    tpb = seq // tm
    return pl.pallas_call(
        _norm_body,
        out_shape=jax.ShapeDtypeStruct((n, d), BF16),
        grid=(n // tm,),
        in_specs=[
            pl.BlockSpec((tm, d), lambda i: (i, 0)),
            pl.BlockSpec((1, SUBLANES, d), lambda i: (i // tpb, 0, 0)),
            _resident((1, d)),
        ],
        out_specs=pl.BlockSpec((tm, d), lambda i: (i, 0)),
        compiler_params=_cparams(("arbitrary",)),
        name="s5_norm",
    )(h, modr, ng)


def _s5_prep(lam_re, lam_im, log_dt, b_re, b_im, c_re, c_im, chunk):
    hi = lax.Precision.HIGHEST
    dt = jnp.exp(log_dt)[:, None]
    mag = jnp.exp(lam_re * dt)
    lb_re = mag * jnp.cos(lam_im * dt)
    lb_im = mag * jnp.sin(lam_im * dt)
    den = lam_re * lam_re + lam_im * lam_im
    f_re = ((lb_re - 1) * lam_re + lb_im * lam_im) / den
    f_im = (lb_im * lam_re - (lb_re - 1) * lam_im) / den
    bb_re = f_re[..., None] * b_re - f_im[..., None] * b_im
    bb_im = f_re[..., None] * b_im + f_im[..., None] * b_re
    g, p, hh = bb_re.shape
    prs, pis = [jnp.ones_like(lb_re)], [jnp.zeros_like(lb_im)]
    for _ in range(chunk):
        pr, pi = prs[-1], pis[-1]
        prs.append(pr * lb_re - pi * lb_im)
        pis.append(pr * lb_im + pi * lb_re)
    pr = jnp.stack(prs)
    pi = jnp.stack(pis)
    kmat = (jnp.einsum('ghp,tgp,gpk->tghk', c_re, pr[:chunk], bb_re, precision=hi)
            - jnp.einsum('ghp,tgp,gpk->tghk', c_re, pi[:chunk], bb_im, precision=hi)
            - jnp.einsum('ghp,tgp,gpk->tghk', c_im, pr[:chunk], bb_im, precision=hi)
            - jnp.einsum('ghp,tgp,gpk->tghk', c_im, pi[:chunk], bb_re, precision=hi))
    prr = pr[chunk - 1::-1][:chunk]
    pir = pi[chunk - 1::-1][:chunk]
    w_re = prr[..., None] * bb_re[None] - pir[..., None] * bb_im[None]
    w_im = prr[..., None] * bb_im[None] + pir[..., None] * bb_re[None]
    w_in = jnp.concatenate([w_re.transpose(1, 0, 3, 2), w_im.transpose(1, 0, 3, 2)], axis=-1)
    w_in = w_in.reshape(g, chunk * hh, 2 * p)
    p1r = pr[1:chunk + 1][:, :, None, :]
    p1i = pi[1:chunk + 1][:, :, None, :]
    co_r = c_re[None] * p1r - c_im[None] * p1i
    co_i = -c_re[None] * p1i - c_im[None] * p1r
    w_out = jnp.concatenate([co_r.transpose(1, 3, 0, 2), co_i.transpose(1, 3, 0, 2)], axis=1)
    w_out = w_out.reshape(g, 2 * p, chunk * hh)
    return kmat, w_in, w_out, pr[chunk], pi[chunk]


def _s5_compact_blocks(kmat, w_in, w_out, chunk, hh, p):
    g = kmat.shape[1]
    gpb = LANES // hh
    nlb = g // gpb
    eye = jnp.eye(gpb, dtype=F32)
    k6 = kmat.reshape(chunk, nlb, gpb, hh, hh).transpose(1, 0, 2, 4, 3)
    bd = (k6[:, :, :, :, None, :] * eye[None, None, :, None, :, None]).reshape(
        nlb, chunk, LANES, LANES)
    wi = w_in.reshape(nlb, gpb, chunk, hh, 2, p).transpose(0, 2, 4, 1, 3, 5)
    wi = wi.reshape(nlb, chunk, 2, LANES, p)
    wi_c = jnp.concatenate([wi] * (LANES // p), axis=-1)
    wo_c = w_out.reshape(nlb, gpb, 2, p, chunk * hh).transpose(0, 2, 1, 3, 4)
    t_i = jnp.arange(chunk)
    h_i = jnp.arange(hh)
    rep = ((t_i[:, None, None, None, None] == t_i[None, None, :, None, None])
           & (h_i[None, :, None, None, None] == h_i[None, None, None, None, :]))
    rep = jnp.broadcast_to(rep, (chunk, hh, chunk, gpb, hh)).reshape(chunk * hh, chunk * LANES)
    return bd.astype(BF16), wi_c.astype(BF16), wo_c.astype(BF16), rep.astype(BF16)


def _fold_chunks(u_ref, ubuf, ucat, chunk):
    ubuf[...] = u_ref[...].astype(F32)
    for s in range(chunk):
        ucat[:, s * LANES:(s + 1) * LANES] = ubuf[:, s, :].astype(BF16)


def _s5_state_body(u_ref, wic_ref, ore_ref, oim_ref, ubuf, ucat, wvm, *, chunk, half, hh, p):
    @pl.when(pl.program_id(1) == 0)
    def _():
        row_g = lax.broadcasted_iota(jnp.int32, (LANES, LANES), 0) // hh
        lane_g = lax.broadcasted_iota(jnp.int32, (LANES, LANES), 1) // p
        for c in range(half // LANES):
            mask = row_g == (LANES // p) * c + lane_g
            for s in range(chunk):
                for q in range(2):
                    lo = q * half + c * LANES
                    wvm[s * LANES:(s + 1) * LANES, lo:lo + LANES] = jnp.where(
                        mask, wic_ref[0, s, q], jnp.zeros((LANES, LANES), BF16))

    _fold_chunks(u_ref, ubuf, ucat, chunk)
    s = jnp.dot(ucat[...], wvm[...], preferred_element_type=F32)
    ore_ref[...] = s[:, 0:half]
    oim_ref[...] = s[:, half:2 * half]


def _s5_state_call(u3, wi_c, hh, p):
    nc, chunk, d = u3.shape
    nlb = d // LANES
    half = LANES // hh * p
    gp = nlb * half
    cb = _largest_divisor(nc, (256, 128, 64, 32, 16, 8))
    return pl.pallas_call(
        functools.partial(_s5_state_body, chunk=chunk, half=half, hh=hh, p=p),
        out_shape=(jax.ShapeDtypeStruct((nc, gp), F32), jax.ShapeDtypeStruct((nc, gp), F32)),
        grid=(nlb, nc // cb),
        in_specs=[
            pl.BlockSpec((cb, chunk, LANES), lambda j, i: (i, 0, j)),
            pl.BlockSpec((1, chunk, 2, LANES, LANES), lambda j, i: (j, 0, 0, 0, 0)),
        ],
        out_specs=(pl.BlockSpec((cb, half), lambda j, i: (i, j)),
                   pl.BlockSpec((cb, half), lambda j, i: (i, j))),
        scratch_shapes=[pltpu.VMEM((cb, chunk, LANES), F32), pltpu.VMEM((cb, chunk * LANES), BF16),
                        pltpu.VMEM((chunk * LANES, 2 * half), BF16)],
        compiler_params=_cparams(("arbitrary", "arbitrary")),
        name="s5_chunk_state",
    )(u3, wi_c)


def _s5_scan_body(s_ref, a_ref, o_ref, carry, *, cb):
    @pl.when(pl.program_id(1) == 0)
    def _():
        carry[...] = jnp.zeros_like(carry)

    ar = a_ref[0]
    ai = a_ref[1]

    def step(c, st):
        xr, xi = st
        o_ref[0, c] = xr
        o_ref[1, c] = xi
        return (ar * xr - ai * xi + s_ref[0, c], ar * xi + ai * xr + s_ref[1, c])

    xr, xi = lax.fori_loop(0, cb, step, (carry[0], carry[1]))
    carry[0] = xr
    carry[1] = xi


def _s5_scan_call(s4, a2, bsz):
    _, nc, r, _ = s4.shape
    ncb = nc // bsz
    cb = _largest_divisor(ncb, (64, 32, 16, 8, 4, 2, 1))
    nb = ncb // cb
    return pl.pallas_call(
        functools.partial(_s5_scan_body, cb=cb),
        out_shape=jax.ShapeDtypeStruct(s4.shape, F32),
        grid=(bsz, nb),
        in_specs=[
            pl.BlockSpec((2, cb, r, LANES), lambda b, j: (0, b * nb + j, 0, 0)),
            _resident(a2.shape),
        ],
        out_specs=pl.BlockSpec((2, cb, r, LANES), lambda b, j: (0, b * nb + j, 0, 0)),
        scratch_shapes=[pltpu.VMEM((2, r, LANES), F32)],
        compiler_params=_cparams(("arbitrary", "arbitrary")),
        name="s5_state_scan",
    )(s4, a2)


def _s5_out_body(u_ref, bd_ref, xre_ref, xim_ref, woc_ref, rep_ref, o_ref, ubuf, ucat, ybuf, tvm, wovm,
                 *, chunk, half, hh, p):
    @pl.when(pl.program_id(1) == 0)
    def _():
        zero = jnp.zeros((LANES, LANES), BF16)
        for s in range(chunk):
            for t in range(chunk):
                tvm[s * LANES:(s + 1) * LANES, t * LANES:(t + 1) * LANES] = (
                    bd_ref[0, t - s] if t >= s else zero)
        lane_g = lax.broadcasted_iota(jnp.int32, (p, chunk * LANES), 1) % LANES // hh
        gpb = LANES // hh
        for q in range(2):
            for a in range(gpb):
                spread = jnp.dot(woc_ref[0, q, a], rep_ref[...], preferred_element_type=F32)
                wovm[(q * gpb + a) * p:(q * gpb + a + 1) * p, :] = jnp.where(
                    lane_g == a, spread, 0.0).astype(BF16)

    _fold_chunks(u_ref, ubuf, ucat, chunk)
    y = jnp.dot(ucat[...], tvm[...], preferred_element_type=F32)
    y = y + jnp.dot(xre_ref[...], wovm[0:half, :], preferred_element_type=F32)
    y = y + jnp.dot(xim_ref[...], wovm[half:2 * half, :], preferred_element_type=F32)
    for t in range(chunk):
        ybuf[:, t, :] = y[:, t * LANES:(t + 1) * LANES]
    o_ref[...] = ybuf[...].astype(BF16)


def _s5_out_call(u3, bd, xs_re, xs_im, wo_c, rep, hh, p):
    nc, chunk, d = u3.shape
    nlb = d // LANES
    gpb = LANES // hh
    half = gpb * p
    wide = chunk * LANES
    cb = _largest_divisor(nc, (256, 128, 64, 32, 16, 8))
    return pl.pallas_call(
        functools.partial(_s5_out_body, chunk=chunk, half=half, hh=hh, p=p),
        out_shape=jax.ShapeDtypeStruct((nc, chunk, d), BF16),
        grid=(nlb, nc // cb),
        in_specs=[
            pl.BlockSpec((cb, chunk, LANES), lambda j, i: (i, 0, j)),
            pl.BlockSpec((1, chunk, LANES, LANES), lambda j, i: (j, 0, 0, 0)),
            pl.BlockSpec((cb, half), lambda j, i: (i, j)),
            pl.BlockSpec((cb, half), lambda j, i: (i, j)),
            pl.BlockSpec((1, 2, gpb, p, chunk * hh), lambda j, i: (j, 0, 0, 0, 0)),
            _resident(rep.shape),
        ],
        out_specs=pl.BlockSpec((cb, chunk, LANES), lambda j, i: (i, 0, j)),
        scratch_shapes=[pltpu.VMEM((cb, chunk, LANES), F32), pltpu.VMEM((cb, wide), BF16),
                        pltpu.VMEM((cb, chunk, LANES), F32), pltpu.VMEM((wide, wide), BF16),
                        pltpu.VMEM((2 * half, wide), BF16)],
        compiler_params=_cparams(("arbitrary", "arbitrary")),
        name="s5_chunk_out",
    )(u3, bd, xs_re, xs_im, wo_c, rep)


def _s5_post_body(h_ref, y_ref, modr_ref, ng_ref, dsk_ref, wglu_ref, o_ref, *, d):
    x = h_ref[...]
    m = modr_ref[0]
    u = _norm_mod(x, ng_ref[...], m[0:1], m[1:2])
    yy = jax.nn.gelu(y_ref[...].astype(F32) + dsk_ref[...] * u)
    r = jnp.dot(yy.astype(BF16), wglu_ref[...], preferred_element_type=F32)
    o_ref[...] = x + m[2:3] * (r[:, 0:d] * jax.nn.sigmoid(r[:, d:2 * d]))


def _s5_post_call(h, y, modr, ng, d_skip, w_glu, seq):
    n, d = h.shape
    tm = min(ROW_TILE, seq)
    tpb = seq // tm
    return pl.pallas_call(
        functools.partial(_s5_post_body, d=d),
        out_shape=jax.ShapeDtypeStruct((n, d), F32),
        grid=(n // tm,),
        in_specs=[
            pl.BlockSpec((tm, d), lambda i: (i, 0)),
            pl.BlockSpec((tm, d), lambda i: (i, 0)),
            pl.BlockSpec((1, SUBLANES, d), lambda i: (i // tpb, 0, 0)),
            _resident((1, d)),
            _resident((1, d)),
            _resident(w_glu.shape),
        ],
        out_specs=pl.BlockSpec((tm, d), lambda i: (i, 0)),
        compiler_params=_cparams(("arbitrary",)),
        name="s5_glu",
    )(h, y, modr, ng, d_skip, w_glu.astype(BF16))


def _s5_mixer(h, modr, ng, lam_re, lam_im, log_dt, b_re, b_im, c_re, c_im, d_skip, w_glu,
              bsz, seq):
    n, d = h.shape
    g, p, hh = b_re.shape
    chunk = min(S5_CHUNK, seq)
    nc = n // chunk
    kmat, w_in, w_out, a_re, a_im = _s5_prep(lam_re, lam_im, log_dt, b_re, b_im, c_re, c_im, chunk)
    bd, wi_c, wo_c, rep = _s5_compact_blocks(kmat, w_in, w_out, chunk, hh, p)
    u3 = _norm_call(h, modr, ng, seq).reshape(nc, chunk, d)
    s_re, s_im = _s5_state_call(u3, wi_c, hh, p)
    r = g * p // LANES
    s4 = jnp.stack([s_re, s_im]).reshape(2, nc, r, LANES)
    a2 = jnp.stack([a_re, a_im]).reshape(2, r, LANES)
    xs4 = _s5_scan_call(s4, a2, bsz)
    xs_re = xs4[0].reshape(nc, g * p).astype(BF16)
    xs_im = xs4[1].reshape(nc, g * p).astype(BF16)
    y = _s5_out_call(u3, bd, xs_re, xs_im, wo_c, rep, hh, p).reshape(n, d)
    return _s5_post_call(h, y, modr, ng, d_skip.reshape(1, d), w_glu, seq)


def _router_body(h_ref, modr_ref, ng_ref, whi_ref, wlo_ref, rb_ref, u_ref, meta_ref, cnt_ref,
                 carry, *, n_exp, tm):
    @pl.when(pl.program_id(0) == 0)
    def _():
        carry[...] = jnp.zeros_like(carry)

    m = modr_ref[0]
    u = _norm_mod(h_ref[...], ng_ref[...], m[3:4], m[4:5])
    u_ref[...] = u
    uh = u.astype(BF16)
    ul = (u - uh.astype(F32)).astype(BF16)
    whi = whi_ref[...]
    logits = (jnp.dot(uh, whi, preferred_element_type=F32)
              + jnp.dot(ul, whi, preferred_element_type=F32)
              + jnp.dot(uh, wlo_ref[...], preferred_element_type=F32)) + rb_ref[...]
    lane = lax.broadcasted_iota(jnp.int32, (tm, LANES), 1)
    lane_f = lane.astype(F32)
    l = jnp.where(lane < n_exp, logits, NEG_BIG)
    tops, idxs, sels = [], [], []
    for _ in range(TOP_K):
        mk = jnp.max(l, axis=-1, keepdims=True)
        ik = jnp.min(jnp.where(l == mk, lane_f, 1.0e9), axis=-1, keepdims=True)
        sk = lane_f == ik
        l = jnp.where(sk, NEG_BIG, l)
        tops.append(mk)
        idxs.append(ik)
        sels.append(sk)
    es = [jnp.exp(t - tops[0]) for t in tops]
    den = es[0]
    for e in es[1:]:
        den = den + e
    chosen = jnp.zeros((tm, LANES), F32)
    for sk in sels:
        chosen = jnp.where(sk, 1.0, chosen)
    row = lax.broadcasted_iota(jnp.int32, (tm, tm), 0)
    col = lax.broadcasted_iota(jnp.int32, (tm, tm), 1)
    tri = jnp.where(row > col, 1.0, 0.0).astype(BF16)
    rank_mat = jnp.dot(tri, chosen.astype(BF16), preferred_element_type=F32) + carry[0:1, :]
    meta = jnp.zeros((tm, LANES), F32)
    for k in range(TOP_K):
        rk = jnp.sum(jnp.where(sels[k], rank_mat, 0.0), axis=-1, keepdims=True)
        meta = jnp.where(lane == k, idxs[k], meta)
        meta = jnp.where(lane == TOP_K + k, rk, meta)
        meta = jnp.where(lane == 2 * TOP_K + k, es[k] / den, meta)
    meta_ref[...] = meta
    carry[...] = carry[...] + jnp.sum(chosen, axis=0, keepdims=True)
    cnt_ref[...] = carry[...]


def _router_call(h, modr, ng, w_router, b_router, seq):
    n, d = h.shape
    n_exp = w_router.shape[1]
    tm = min(ROW_TILE, seq)
    tpb = seq // tm
    wpad = jnp.zeros((d, LANES), F32).at[:, :n_exp].set(w_router)
    whi = wpad.astype(BF16)
    wlo = (wpad - whi.astype(F32)).astype(BF16)
    rb = jnp.zeros((1, LANES), F32).at[0, :n_exp].set(b_router)
    return pl.pallas_call(
        functools.partial(_router_body, n_exp=n_exp, tm=tm),
        out_shape=(
            jax.ShapeDtypeStruct((n, d), F32),
            jax.ShapeDtypeStruct((n, LANES), F32),
            jax.ShapeDtypeStruct((SUBLANES, LANES), F32),
        ),
        grid=(n // tm,),
        in_specs=[
            pl.BlockSpec((tm, d), lambda i: (i, 0)),
            pl.BlockSpec((1, SUBLANES, d), lambda i: (i // tpb, 0, 0)),
            _resident((1, d)),
            _resident(whi.shape),
            _resident(wlo.shape),
            _resident(rb.shape),
        ],
        out_specs=(
            pl.BlockSpec((tm, d), lambda i: (i, 0)),
            pl.BlockSpec((tm, LANES), lambda i: (i, 0)),
            pl.BlockSpec((SUBLANES, LANES), lambda i: (0, 0)),
        ),
        scratch_shapes=[pltpu.VMEM((SUBLANES, LANES), F32)],
        compiler_params=_cparams(("arbitrary",)),
        name="moe_router",
    )(h, modr, ng, whi, wlo, rb)


def _row_copy(src_ref, src_row, dst_ref, dst_row, sem):
    return pltpu.make_async_copy(src_ref.at[pl.ds(src_row, 1)], dst_ref.at[pl.ds(dst_row, 1)], sem)


def _rows_wait(hbm_ref, rows, sem):
    pltpu.make_async_copy(hbm_ref.at[pl.ds(0, rows)], hbm_ref.at[pl.ds(0, rows)], sem).wait()


def _dispatch_body(npad_ref, padrow_ref, dest_ref, u_ref, xs_ref, zrow, sem, *, td, pp):
    i = pl.program_id(0)
    zrow[...] = jnp.zeros_like(zrow)
    n_pad = jnp.clip(npad_ref[0] - i * pp, 0, pp)

    def issue(r, c):
        for k in range(TOP_K):
            _row_copy(u_ref, r, xs_ref, dest_ref[TOP_K * r + k], sem).start()
        return c

    lax.fori_loop(0, td, issue, 0, unroll=DMA_UNROLL)

    def issue_pad(j, c):
        _row_copy(zrow, 0, xs_ref, padrow_ref[i * pp + j], sem).start()
        return c

    lax.fori_loop(0, n_pad, issue_pad, 0)

    _rows_wait(xs_ref, TOP_K * td, sem)

    def wait_one(j, c):
        _row_copy(zrow, 0, xs_ref, 0, sem).wait()
        return c

    lax.fori_loop(0, n_pad, wait_one, 0)


def _dispatch_call(u, dest, pad_rows, n_pad, n_slots):
    n, d = u.shape
    td = min(ROW_TILE, n)
    steps = n // td
    pp = -(-pad_rows.shape[0] // steps)
    return pl.pallas_call(
        functools.partial(_dispatch_body, td=td, pp=pp),
        out_shape=jax.ShapeDtypeStruct((n_slots, d), F32),
        grid_spec=pltpu.PrefetchScalarGridSpec(
            num_scalar_prefetch=2,
            grid=(steps,),
            in_specs=[
                pl.BlockSpec((TOP_K * td,), lambda i, npad, prow: (i,), memory_space=pltpu.SMEM),
                pl.BlockSpec((td, d), lambda i, npad, prow: (i, 0)),
            ],
            out_specs=pl.BlockSpec(memory_space=pl.ANY),
            scratch_shapes=[pltpu.VMEM((SUBLANES, d), F32), pltpu.SemaphoreType.DMA(())],
        ),
        compiler_params=pltpu.CompilerParams(dimension_semantics=("arbitrary",),
                                             vmem_limit_bytes=VMEM_LIMIT, has_side_effects=True,
                                             disable_bounds_checks=True),
        name="moe_dispatch",
    )(n_pad, pad_rows, dest, u)


def _expert_body(bexp_ref, nused_ref, x_ref, wg_ref, bg_ref, wu_ref, bu_ref, wd_ref, bd_ref, o_ref):
    @pl.when(pl.program_id(0) < nused_ref[0])
    def _():
        x = x_ref[...].astype(BF16)
        g = jnp.dot(x, wg_ref[0, 0], preferred_element_type=F32) + bg_ref[0, 0]
        g = jnp.minimum(g, SWIGLU_LIMIT)
        v = jnp.dot(x, wu_ref[0, 0], preferred_element_type=F32) + bu_ref[0, 0]
        v = jnp.clip(v, -SWIGLU_LIMIT, SWIGLU_LIMIT)
        a = g * jax.nn.sigmoid(SWIGLU_ALPHA * g) * (v + 1.0)
        o_ref[...] = jnp.dot(a.astype(BF16), wd_ref[0, 0], preferred_element_type=F32) + bd_ref[0, 0]


def _expert_call(xs, block_exp, n_used, layer, wg, bg, wu, bu, wd, bd):
    n_slots, d = xs.shape
    depth, n_exp, _, f = wg.shape
    te = EXPERT_TILE
    nblk = n_slots // te

    def xmap(b, be, nu):
        return (jnp.minimum(b, nu[0] - 1), 0)

    def wmap(b, be, nu):
        return (layer, be[b], 0, 0)

    return pl.pallas_call(
        _expert_body,
        out_shape=jax.ShapeDtypeStruct((n_slots, d), F32),
        grid_spec=pltpu.PrefetchScalarGridSpec(
            num_scalar_prefetch=2,
            grid=(nblk,),
            in_specs=[
                pl.BlockSpec((te, d), xmap),
                pl.BlockSpec((1, 1, d, f), wmap),
                pl.BlockSpec((1, 1, 1, f), wmap),
                pl.BlockSpec((1, 1, d, f), wmap),
                pl.BlockSpec((1, 1, 1, f), wmap),
                pl.BlockSpec((1, 1, f, d), wmap),
                pl.BlockSpec((1, 1, 1, d), wmap),
            ],
            out_specs=pl.BlockSpec((te, d), xmap),
        ),
        compiler_params=_cparams(("arbitrary",)),
        name="moe_experts",
    )(block_exp, n_used, xs, wg, bg.reshape(depth, n_exp, 1, f), wu, bu.reshape(depth, n_exp, 1, f),
      wd, bd.reshape(depth, n_exp, 1, d))


def _combine_body(dest_ref, dnext_ref, h_ref, meta_ref, modr_ref, fg_ref, ys_ref, o_ref, buf, sem,
                  *, tc, final):
    i = pl.program_id(0)
    slot = i % 2

    def gather(d_ref, s):
        def issue(r, c):
            for k in range(TOP_K):
                pltpu.make_async_copy(ys_ref.at[pl.ds(d_ref[TOP_K * r + k], 1)],
                                      buf.at[s, k, pl.ds(r, 1)], sem.at[s]).start()
            return c

        lax.fori_loop(0, tc, issue, 0, unroll=DMA_UNROLL)

    @pl.when(i == 0)
    def _():
        gather(dest_ref, 0)

    @pl.when(i + 1 < pl.num_programs(0))
    def _():
        gather(dnext_ref, 1 - slot)

    _rows_wait(ys_ref, TOP_K * tc, sem.at[slot])

    meta = meta_ref[...]
    acc = meta[:, 2 * TOP_K:2 * TOP_K + 1] * buf[slot, 0]
    for k in range(1, TOP_K):
        acc = acc + meta[:, 2 * TOP_K + k:2 * TOP_K + k + 1] * buf[slot, k]
    out = h_ref[...] + modr_ref[0][5:6] * acc
    if final:
        ms = jnp.mean(out * out, axis=-1, keepdims=True)
        out = out * lax.rsqrt(ms + NORM_EPS) * fg_ref[...]
    o_ref[...] = out


def _combine_call(h, meta, modr, final_g, ys, dest, seq, final):
    n, d = h.shape
    tc = min(ROW_TILE // 2, seq)
    tpb = seq // tc
    steps = n // tc
    return pl.pallas_call(
        functools.partial(_combine_body, tc=tc, final=final),
        out_shape=jax.ShapeDtypeStruct((n, d), F32),
        grid=(steps,),
        in_specs=[
            pl.BlockSpec((TOP_K * tc,), lambda i: (i,), memory_space=pltpu.SMEM),
            pl.BlockSpec((TOP_K * tc,), lambda i: (jnp.minimum(i + 1, steps - 1),),
                         memory_space=pltpu.SMEM),
            pl.BlockSpec((tc, d), lambda i: (i, 0)),
            pl.BlockSpec((tc, LANES), lambda i: (i, 0)),
            pl.BlockSpec((1, SUBLANES, d), lambda i: (i // tpb, 0, 0)),
            _resident((1, d)),
            pl.BlockSpec(memory_space=pl.ANY),
        ],
        out_specs=pl.BlockSpec((tc, d), lambda i: (i, 0)),
        scratch_shapes=[pltpu.VMEM((2, TOP_K, tc, d), F32), pltpu.SemaphoreType.DMA((2,))],
        compiler_params=pltpu.CompilerParams(dimension_semantics=("arbitrary",),
                                             vmem_limit_bytes=VMEM_LIMIT,
                                             disable_bounds_checks=True),
        name="moe_combine",
    )(dest, dest, h, meta, modr, final_g, ys)


def _moe(h, modr, ng, w_router, b_router, layer, wg, bg, wu, bu, wd, bd, final_g, seq, final):
    n, d = h.shape
    n_exp = w_router.shape[1]
    te = EXPERT_TILE
    n_slots = n * TOP_K + n_exp * te
    nblk = n_slots // te
    u, meta, cnt = _router_call(h, modr, ng, w_router, b_router, seq)

    idx = meta[:, 0:TOP_K].astype(jnp.int32)
    rank = meta[:, TOP_K:2 * TOP_K].astype(jnp.int32)
    counts = cnt[0, :n_exp].astype(jnp.int32)
    padded = (counts + te - 1) // te * te
    pad_ends = jnp.cumsum(padded)
    pad_starts = pad_ends - padded
    dest = (pad_starts[idx] + rank).reshape(-1)
    n_used = pad_ends[-1] // te
    blk = jnp.minimum(jnp.arange(nblk, dtype=jnp.int32), n_used - 1)
    block_exp = jnp.minimum(jnp.sum(pad_ends[None, :] <= (blk * te)[:, None], axis=1),
                            n_exp - 1).astype(jnp.int32)
    within = jnp.arange(te, dtype=jnp.int32)[None, :]
    n_fill = (padded - counts)[:, None]
    fill_rows = jnp.where(within < n_fill, (pad_starts + counts)[:, None] + within, n_slots)
    pad_rows = jnp.sort(fill_rows.reshape(-1)).astype(jnp.int32)
    n_pad = jnp.sum(padded - counts).astype(jnp.int32).reshape(1)

    xs = _dispatch_call(u, dest, pad_rows, n_pad, n_slots)
    ys = _expert_call(xs, block_exp, n_used.astype(jnp.int32).reshape(1), layer,
                      wg, bg, wu, bu, wd, bd)
    return _combine_call(h, meta, modr, final_g, ys, dest, seq, final)


def kernel(x, c, ada_w, ada_b, norm_mix_g, norm_ffn_g, cv_w_in, cv_sc_w, cv_cf_w, cv_cf_b, cv_ln_g, cv_ln_b, cv_w_out, s5_lam_re, s5_lam_im, s5_log_dt, s5_b_re, s5_b_im, s5_c_re, s5_c_im, s5_d, s5_w_glu, router_w, router_b, exp_w_gate, exp_b_gate, exp_w_up, exp_b_up, exp_w_down, exp_b_down, final_g):
    bsz, seq, d = x.shape
    depth = ada_w.shape[0]
    n = bsz * seq
    assert bsz <= SUBLANES
    h = x.reshape(n, d)
    mod = _mod_call(c, ada_w, ada_b)
    fg = final_g.reshape(1, d)
    wg16 = exp_w_gate.astype(BF16)
    wu16 = exp_w_up.astype(BF16)
    wd16 = exp_w_down.astype(BF16)
    for layer in range(depth):
        modr = jnp.zeros((bsz, SUBLANES, d), F32).at[:, :6].set(mod[layer, :bsz].reshape(bsz, 6, d))
        ng = norm_mix_g[layer].reshape(1, d)
        j = layer // 2
        if layer % 2 == 0:
            h = _conv_mixer_call(h, modr, ng, cv_w_in[j], cv_sc_w[j], cv_cf_w[j], cv_cf_b[j],
                                 cv_ln_g[j], cv_ln_b[j], cv_w_out[j], bsz, seq)
        else:
            h = _s5_mixer(h, modr, ng, s5_lam_re[j], s5_lam_im[j], s5_log_dt[j], s5_b_re[j],
                          s5_b_im[j], s5_c_re[j], s5_c_im[j], s5_d[j], s5_w_glu[j], bsz, seq)
        h = _moe(h, modr, norm_ffn_g[layer].reshape(1, d), router_w[layer], router_b[layer],
                 layer, wg16, exp_b_gate, wu16, exp_b_up, wd16, exp_b_down, fg, seq,
                 layer == depth - 1)
    return h.reshape(bsz, seq, d)
```

```python
import functools

import jax
import jax.numpy as jnp
from jax import lax
from jax.experimental import pallas as pl
from jax.experimental.pallas import tpu as pltpu

F32 = jnp.float32
BF16 = jnp.bfloat16
NORM_EPS = 1e-5
TOP_K = 4
SWIGLU_LIMIT = 7.0
SWIGLU_ALPHA = 1.702
LANES = 128
SUBLANES = 8
VMEM_LIMIT = 56 * 1024 * 1024
EXPERT_VMEM_LIMIT = 60 * 1024 * 1024
S5_CHUNK = 16
ROW_TILE = 256
EXPERT_TILE = 256
DMA_UNROLL = 8
NEG_BIG = -3.0e38


def _cparams(sem):
    return pltpu.CompilerParams(dimension_semantics=sem, vmem_limit_bytes=VMEM_LIMIT)


def _largest_divisor(n, candidates):
    for c in candidates:
        if n % c == 0:
            return c
    return n


def _norm_mod(x, g, shift, scale):
    ms = jnp.mean(x * x, axis=-1, keepdims=True)
    return x * lax.rsqrt(ms + NORM_EPS) * g * (1.0 + scale) + shift


def _resident(shape):
    nd = len(shape)
    return pl.BlockSpec(shape, lambda *_: (0,) * nd, pipeline_mode=pl.Buffered(1))


def _mod_body(c_ref, w_ref, b_ref, o_ref):
    c = c_ref[...]
    ca = (c * jax.nn.sigmoid(c)).astype(BF16)
    o_ref[0] = jnp.dot(ca, w_ref[0].astype(BF16), preferred_element_type=F32) + b_ref[0]


def _mod_call(c, ada_w, ada_b):
    depth, d, w = ada_w.shape
    b = c.shape[0]
    c8 = jnp.zeros((SUBLANES, d), F32).at[:b].set(c)
    tn = _largest_divisor(w, (1536, 1024, 768, 512, 384, 256, 128))
    return pl.pallas_call(
        _mod_body,
        out_shape=jax.ShapeDtypeStruct((depth, SUBLANES, w), F32),
        grid=(depth, w // tn),
        in_specs=[
            pl.BlockSpec((SUBLANES, d), lambda l, j: (0, 0)),
            pl.BlockSpec((1, d, tn), lambda l, j: (l, 0, j)),
            pl.BlockSpec((1, 1, tn), lambda l, j: (l, 0, j)),
        ],
        out_specs=pl.BlockSpec((1, SUBLANES, tn), lambda l, j: (l, 0, j)),
        compiler_params=_cparams(("arbitrary", "arbitrary")),
        name="adaln_mod",
    )(c8, ada_w, ada_b.reshape(depth, 1, w))


def _conv_body(h_ref, modr_ref, ng_ref, win_ref, scw_ref, cfw_ref, cfv_ref, wout_ref, o_ref,
               gbuf, cvbuf, ycat, wbc, *, tm, sc, cf, ka, kb, ha, hb, rc):
    @pl.when((pl.program_id(0) == 0) & (pl.program_id(1) == 0))
    def _():
        for k in range(kb):
            wbc[k] = jnp.broadcast_to(cfw_ref[k:k + 1, :], (SUBLANES, cf))

    @pl.when(pl.program_id(1) == 0)
    def _():
        gbuf[0, 0:hb, :] = jnp.zeros((hb, cf), F32)
        cvbuf[0:ha, :] = jnp.zeros((ha, sc), F32)

    x = h_ref[...]
    m = modr_ref[0]
    u16 = _norm_mod(x, ng_ref[...], m[0:1], m[1:2]).astype(BF16)

    zb = jnp.dot(u16, win_ref[:, 3 * sc:3 * sc + 2 * cf], preferred_element_type=F32)
    gbuf[0, hb:hb + tm, :] = zb[:, 0:cf] * jax.nn.sigmoid(zb[:, cf:2 * cf])
    span = tm + hb - SUBLANES
    for b in range(1, SUBLANES):
        for c0 in range(0, span, 64):
            c1 = min(c0 + 64, span)
            gbuf[b, c0:c1, :] = gbuf[0, b + c0:b + c1, :]

    za = jnp.dot(u16, win_ref[:, 0:3 * sc], preferred_element_type=F32)
    cv = za[:, sc:2 * sc] * za[:, 2 * sc:3 * sc]
    cvbuf[ha:ha + tm, :] = cv
    acc = scw_ref[ka - 1:ka, :] * cv
    for k in range(ka - 1):
        off = ha - (ka - 1) + k
        acc = acc + scw_ref[k:k + 1, :] * cvbuf[off:off + tm, :]
    ycat[:, 0:sc] = (za[:, 0:sc] * acc).astype(BF16)
    y = jnp.dot(ycat[:, 0:sc], wout_ref[0:sc, :], preferred_element_type=F32)

    cf_b = cfv_ref[0:1, :]
    ln_g = cfv_ref[1:2, :]
    ln_b = cfv_ref[2:3, :]
    for r0 in range(0, tm, rc):
        rows = range(r0, r0 + rc, SUBLANES)
        w_last = wbc[kb - 1]
        accs = [w_last * gbuf[0, hb + r:hb + r + SUBLANES, :] for r in rows]
        for k in range(kb - 1):
            off = hb - (kb - 1) + k
            al = off // SUBLANES * SUBLANES
            wk = wbc[k]
            for c, r in enumerate(rows):
                accs[c] = accs[c] + wk * gbuf[off % SUBLANES, al + r:al + r + SUBLANES, :]
        a = jnp.concatenate(accs, axis=0) + cf_b
        mu = jnp.mean(a, axis=-1, keepdims=True)
        ac = a - mu
        var = jnp.mean(ac * ac, axis=-1, keepdims=True)
        yn = ac * lax.rsqrt(var + NORM_EPS) * ln_g + ln_b
        ycat[r0:r0 + rc, sc:sc + cf] = (yn * jax.nn.sigmoid(yn)).astype(BF16)

    cvbuf[0:ha, :] = cvbuf[tm:tm + ha, :]
    gbuf[0, 0:hb, :] = gbuf[0, tm:tm + hb, :]

    y = y + jnp.dot(ycat[:, sc:sc + cf], wout_ref[sc:sc + cf, :], preferred_element_type=F32)
    o_ref[...] = x + m[2:3] * y


def _conv_mixer_call(h, modr, ng, w_in, sc_w, cf_w, cf_b, ln_g, ln_b, w_out, bsz, seq):
    n, d = h.shape
    ka, sc = sc_w.shape
    kb, cf = cf_w.shape
    ha = -(-(ka - 1) // SUBLANES) * SUBLANES
    hb = -(-(kb - 1) // SUBLANES) * SUBLANES
    tm = min(ROW_TILE, seq)
    assert seq % tm == 0 and tm >= hb and tm >= ha
    ns = seq // tm
    scw = jnp.zeros((SUBLANES, sc), F32).at[:ka].set(sc_w)
    cfw = jnp.zeros((-(-kb // SUBLANES) * SUBLANES, cf), F32).at[:kb].set(cf_w)
    cfv = jnp.zeros((SUBLANES, cf), F32).at[0].set(cf_b).at[1].set(ln_g).at[2].set(ln_b)
    body = functools.partial(_conv_body, tm=tm, sc=sc, cf=cf, ka=ka, kb=kb, ha=ha, hb=hb,
                             rc=min(16, tm))
    return pl.pallas_call(
        body,
        out_shape=jax.ShapeDtypeStruct((n, d), F32),
        grid=(bsz, ns),
        in_specs=[
            pl.BlockSpec((tm, d), lambda b, s: (b * ns + s, 0)),
            pl.BlockSpec((1, SUBLANES, d), lambda b, s: (b, 0, 0)),
            _resident((1, d)),
            _resident(w_in.shape),
            _resident(scw.shape),
            _resident(cfw.shape),
            _resident(cfv.shape),
            _resident(w_out.shape),
        ],
        out_specs=pl.BlockSpec((tm, d), lambda b, s: (b * ns + s, 0)),
        scratch_shapes=[
            pltpu.VMEM((SUBLANES, tm + hb, cf), F32),
            pltpu.VMEM((tm + ha, sc), F32),
            pltpu.VMEM((tm, sc + cf), BF16),
            pltpu.VMEM((cfw.shape[0], SUBLANES, cf), F32),
        ],
        compiler_params=_cparams(("arbitrary", "arbitrary")),
        name="conv_mixer",
    )(h, modr, ng, w_in.astype(BF16), scw, cfw, cfv, w_out.astype(BF16))


def _norm_body(h_ref, modr_ref, ng_ref, o_ref):
    m = modr_ref[0]
    o_ref[...] = _norm_mod(h_ref[...], ng_ref[...], m[0:1], m[1:2]).astype(BF16)


def _norm_call(h, modr, ng, seq):
    n, d = h.shape
    tm = min(2 * ROW_TILE, seq)
    tpb = seq // tm
    return pl.pallas_call(
        _norm_body,
        out_shape=jax.ShapeDtypeStruct((n, d), BF16),
        grid=(n // tm,),
        in_specs=[
            pl.BlockSpec((tm, d), lambda i: (i, 0)),
            pl.BlockSpec((1, SUBLANES, d), lambda i: (i // tpb, 0, 0)),
            _resident((1, d)),
        ],
        out_specs=pl.BlockSpec((tm, d), lambda i: (i, 0)),
        compiler_params=_cparams(("arbitrary",)),
        name="s5_norm",
    )(h, modr, ng)


def _s5_prep(lam_re, lam_im, log_dt, b_re, b_im, c_re, c_im, chunk):
    hi = lax.Precision.HIGHEST
    dt = jnp.exp(log_dt)[:, None]
    mag = jnp.exp(lam_re * dt)
    lb_re = mag * jnp.cos(lam_im * dt)
    lb_im = mag * jnp.sin(lam_im * dt)
    den = lam_re * lam_re + lam_im * lam_im
    f_re = ((lb_re - 1) * lam_re + lb_im * lam_im) / den
    f_im = (lb_im * lam_re - (lb_re - 1) * lam_im) / den
    bb_re = f_re[..., None] * b_re - f_im[..., None] * b_im
    bb_im = f_re[..., None] * b_im + f_im[..., None] * b_re
    g, p, hh = bb_re.shape
    prs, pis = [jnp.ones_like(lb_re)], [jnp.zeros_like(lb_im)]
    for _ in range(chunk):
        pr, pi = prs[-1], pis[-1]
        prs.append(pr * lb_re - pi * lb_im)
        pis.append(pr * lb_im + pi * lb_re)
    pr = jnp.stack(prs)
    pi = jnp.stack(pis)
    kmat = (jnp.einsum('ghp,tgp,gpk->tghk', c_re, pr[:chunk], bb_re, precision=hi)
            - jnp.einsum('ghp,tgp,gpk->tghk', c_re, pi[:chunk], bb_im, precision=hi)
            - jnp.einsum('ghp,tgp,gpk->tghk', c_im, pr[:chunk], bb_im, precision=hi)
            - jnp.einsum('ghp,tgp,gpk->tghk', c_im, pi[:chunk], bb_re, precision=hi))
    prr = pr[chunk - 1::-1][:chunk]
    pir = pi[chunk - 1::-1][:chunk]
    w_re = prr[..., None] * bb_re[None] - pir[..., None] * bb_im[None]
    w_im = prr[..., None] * bb_im[None] + pir[..., None] * bb_re[None]
    w_in = jnp.concatenate([w_re.transpose(1, 0, 3, 2), w_im.transpose(1, 0, 3, 2)], axis=-1)
    w_in = w_in.reshape(g, chunk * hh, 2 * p)
    p1r = pr[1:chunk + 1][:, :, None, :]
    p1i = pi[1:chunk + 1][:, :, None, :]
    co_r = c_re[None] * p1r - c_im[None] * p1i
    co_i = -c_re[None] * p1i - c_im[None] * p1r
    w_out = jnp.concatenate([co_r.transpose(1, 3, 0, 2), co_i.transpose(1, 3, 0, 2)], axis=1)
    w_out = w_out.reshape(g, 2 * p, chunk * hh)
    return kmat, w_in, w_out, pr[chunk], pi[chunk]


def _s5_compact_blocks(kmat, w_in, w_out, chunk, hh, p):
    g = kmat.shape[1]
    gpb = LANES // hh
    nlb = g // gpb
    eye = jnp.eye(gpb, dtype=F32)
    k6 = kmat.reshape(chunk, nlb, gpb, hh, hh).transpose(1, 0, 2, 4, 3)
    bd = (k6[:, :, :, :, None, :] * eye[None, None, :, None, :, None]).reshape(
        nlb, chunk, LANES, LANES)
    wi = w_in.reshape(nlb, gpb, chunk, hh, 2, p).transpose(0, 2, 4, 1, 3, 5)
    wi = wi.reshape(nlb, chunk, 2, LANES, p)
    wi_c = jnp.concatenate([wi] * (LANES // p), axis=-1)
    wo_c = w_out.reshape(nlb, gpb, 2, p, chunk * hh).transpose(0, 2, 1, 3, 4)
    t_i = jnp.arange(chunk)
    h_i = jnp.arange(hh)
    rep = ((t_i[:, None, None, None, None] == t_i[None, None, :, None, None])
           & (h_i[None, :, None, None, None] == h_i[None, None, None, None, :]))
    rep = jnp.broadcast_to(rep, (chunk, hh, chunk, gpb, hh)).reshape(chunk * hh, chunk * LANES)
    return bd.astype(BF16), wi_c.astype(BF16), wo_c.astype(BF16), rep.astype(BF16)


def _fold_chunks(u_ref, ubuf, ucat, chunk):
    ubuf[...] = u_ref[...].astype(F32)
    for s in range(chunk):
        ucat[:, s * LANES:(s + 1) * LANES] = ubuf[:, s, :].astype(BF16)


def _s5_state_body(u_ref, wic_ref, ore_ref, oim_ref, ubuf, ucat, wvm, *, chunk, half, hh, p):
    @pl.when(pl.program_id(1) == 0)
    def _():
        row_g = lax.broadcasted_iota(jnp.int32, (LANES, LANES), 0) // hh
        lane_g = lax.broadcasted_iota(jnp.int32, (LANES, LANES), 1) // p
        for c in range(half // LANES):
            mask = row_g == (LANES // p) * c + lane_g
            for s in range(chunk):
                for q in range(2):
                    lo = q * half + c * LANES
                    wvm[s * LANES:(s + 1) * LANES, lo:lo + LANES] = jnp.where(
                        mask, wic_ref[0, s, q], jnp.zeros((LANES, LANES), BF16))

    _fold_chunks(u_ref, ubuf, ucat, chunk)
    s = jnp.dot(ucat[...], wvm[...], preferred_element_type=F32)
    ore_ref[...] = s[:, 0:half]
    oim_ref[...] = s[:, half:2 * half]


def _s5_state_call(u3, wi_c, hh, p):
    nc, chunk, d = u3.shape
    nlb = d // LANES
    half = LANES // hh * p
    gp = nlb * half
    cb = _largest_divisor(nc, (256, 128, 64, 32, 16, 8))
    return pl.pallas_call(
        functools.partial(_s5_state_body, chunk=chunk, half=half, hh=hh, p=p),
        out_shape=(jax.ShapeDtypeStruct((nc, gp), F32), jax.ShapeDtypeStruct((nc, gp), F32)),
        grid=(nlb, nc // cb),
        in_specs=[
            pl.BlockSpec((cb, chunk, LANES), lambda j, i: (i, 0, j)),
            pl.BlockSpec((1, chunk, 2, LANES, LANES), lambda j, i: (j, 0, 0, 0, 0)),
        ],
        out_specs=(pl.BlockSpec((cb, half), lambda j, i: (i, j)),
                   pl.BlockSpec((cb, half), lambda j, i: (i, j))),
        scratch_shapes=[pltpu.VMEM((cb, chunk, LANES), F32), pltpu.VMEM((cb, chunk * LANES), BF16),
                        pltpu.VMEM((chunk * LANES, 2 * half), BF16)],
        compiler_params=_cparams(("arbitrary", "arbitrary")),
        name="s5_chunk_state",
    )(u3, wi_c)


def _s5_scan_body(s_ref, a_ref, o_ref, carry, *, cb):
    @pl.when(pl.program_id(1) == 0)
    def _():
        carry[...] = jnp.zeros_like(carry)

    ar = a_ref[0]
    ai = a_ref[1]

    def step(c, st):
        xr, xi = st
        o_ref[0, c] = xr
        o_ref[1, c] = xi
        return (ar * xr - ai * xi + s_ref[0, c], ar * xi + ai * xr + s_ref[1, c])

    xr, xi = lax.fori_loop(0, cb, step, (carry[0], carry[1]))
    carry[0] = xr
    carry[1] = xi


def _s5_scan_call(s4, a2, bsz):
    _, nc, r, _ = s4.shape
    ncb = nc // bsz
    cb = _largest_divisor(ncb, (64, 32, 16, 8, 4, 2, 1))
    nb = ncb // cb
    return pl.pallas_call(
        functools.partial(_s5_scan_body, cb=cb),
        out_shape=jax.ShapeDtypeStruct(s4.shape, F32),
        grid=(bsz, nb),
        in_specs=[
            pl.BlockSpec((2, cb, r, LANES), lambda b, j: (0, b * nb + j, 0, 0)),
            _resident(a2.shape),
        ],
        out_specs=pl.BlockSpec((2, cb, r, LANES), lambda b, j: (0, b * nb + j, 0, 0)),
        scratch_shapes=[pltpu.VMEM((2, r, LANES), F32)],
        compiler_params=_cparams(("arbitrary", "arbitrary")),
        name="s5_state_scan",
    )(s4, a2)


def _s5_out_body(u_ref, bd_ref, xre_ref, xim_ref, woc_ref, rep_ref, o_ref, ubuf, ucat, ybuf, tvm, wovm,
                 *, chunk, half, hh, p):
    @pl.when(pl.program_id(1) == 0)
    def _():
        zero = jnp.zeros((LANES, LANES), BF16)
        for s in range(chunk):
            for t in range(chunk):
                tvm[s * LANES:(s + 1) * LANES, t * LANES:(t + 1) * LANES] = (
                    bd_ref[0, t - s] if t >= s else zero)
        lane_g = lax.broadcasted_iota(jnp.int32, (p, chunk * LANES), 1) % LANES // hh
        gpb = LANES // hh
        for q in range(2):
            for a in range(gpb):
                spread = jnp.dot(woc_ref[0, q, a], rep_ref[...], preferred_element_type=F32)
                wovm[(q * gpb + a) * p:(q * gpb + a + 1) * p, :] = jnp.where(
                    lane_g == a, spread, 0.0).astype(BF16)

    _fold_chunks(u_ref, ubuf, ucat, chunk)
    y = jnp.dot(ucat[...], tvm[...], preferred_element_type=F32)
    y = y + jnp.dot(xre_ref[...], wovm[0:half, :], preferred_element_type=F32)
    y = y + jnp.dot(xim_ref[...], wovm[half:2 * half, :], preferred_element_type=F32)
    for t in range(chunk):
        ybuf[:, t, :] = y[:, t * LANES:(t + 1) * LANES]
    o_ref[...] = ybuf[...].astype(BF16)


def _s5_out_call(u3, bd, xs_re, xs_im, wo_c, rep, hh, p):
    nc, chunk, d = u3.shape
    nlb = d // LANES
    gpb = LANES // hh
    half = gpb * p
    wide = chunk * LANES
    cb = _largest_divisor(nc, (256, 128, 64, 32, 16, 8))
    return pl.pallas_call(
        functools.partial(_s5_out_body, chunk=chunk, half=half, hh=hh, p=p),
        out_shape=jax.ShapeDtypeStruct((nc, chunk, d), BF16),
        grid=(nlb, nc // cb),
        in_specs=[
            pl.BlockSpec((cb, chunk, LANES), lambda j, i: (i, 0, j)),
            pl.BlockSpec((1, chunk, LANES, LANES), lambda j, i: (j, 0, 0, 0)),
            pl.BlockSpec((cb, half), lambda j, i: (i, j)),
            pl.BlockSpec((cb, half), lambda j, i: (i, j)),
            pl.BlockSpec((1, 2, gpb, p, chunk * hh), lambda j, i: (j, 0, 0, 0, 0)),
            _resident(rep.shape),
        ],
        out_specs=pl.BlockSpec((cb, chunk, LANES), lambda j, i: (i, 0, j)),
        scratch_shapes=[pltpu.VMEM((cb, chunk, LANES), F32), pltpu.VMEM((cb, wide), BF16),
                        pltpu.VMEM((cb, chunk, LANES), F32), pltpu.VMEM((wide, wide), BF16),
                        pltpu.VMEM((2 * half, wide), BF16)],
        compiler_params=_cparams(("arbitrary", "arbitrary")),
        name="s5_chunk_out",
    )(u3, bd, xs_re, xs_im, wo_c, rep)


def _s5_post_body(h_ref, y_ref, modr_ref, ng_ref, dsk_ref, wglu_ref, o_ref, *, d):
    x = h_ref[...]
    m = modr_ref[0]
    u = _norm_mod(x, ng_ref[...], m[0:1], m[1:2])
    yy = jax.nn.gelu(y_ref[...].astype(F32) + dsk_ref[...] * u)
    r = jnp.dot(yy.astype(BF16), wglu_ref[...], preferred_element_type=F32)
    o_ref[...] = x + m[2:3] * (r[:, 0:d] * jax.nn.sigmoid(r[:, d:2 * d]))


def _s5_post_call(h, y, modr, ng, d_skip, w_glu, seq):
    n, d = h.shape
    tm = min(ROW_TILE, seq)
    tpb = seq // tm
    return pl.pallas_call(
        functools.partial(_s5_post_body, d=d),
        out_shape=jax.ShapeDtypeStruct((n, d), F32),
        grid=(n // tm,),
        in_specs=[
            pl.BlockSpec((tm, d), lambda i: (i, 0)),
            pl.BlockSpec((tm, d), lambda i: (i, 0)),
            pl.BlockSpec((1, SUBLANES, d), lambda i: (i // tpb, 0, 0)),
            _resident((1, d)),
            _resident((1, d)),
            _resident(w_glu.shape),
        ],
        out_specs=pl.BlockSpec((tm, d), lambda i: (i, 0)),
        compiler_params=_cparams(("arbitrary",)),
        name="s5_glu",
    )(h, y, modr, ng, d_skip, w_glu.astype(BF16))


def _s5_mixer(h, modr, ng, lam_re, lam_im, log_dt, b_re, b_im, c_re, c_im, d_skip, w_glu,
              bsz, seq):
    n, d = h.shape
    g, p, hh = b_re.shape
    chunk = min(S5_CHUNK, seq)
    nc = n // chunk
    kmat, w_in, w_out, a_re, a_im = _s5_prep(lam_re, lam_im, log_dt, b_re, b_im, c_re, c_im, chunk)
    bd, wi_c, wo_c, rep = _s5_compact_blocks(kmat, w_in, w_out, chunk, hh, p)
    u3 = _norm_call(h, modr, ng, seq).reshape(nc, chunk, d)
    s_re, s_im = _s5_state_call(u3, wi_c, hh, p)
    r = g * p // LANES
    s4 = jnp.stack([s_re, s_im]).reshape(2, nc, r, LANES)
    a2 = jnp.stack([a_re, a_im]).reshape(2, r, LANES)
    xs4 = _s5_scan_call(s4, a2, bsz)
    xs_re = xs4[0].reshape(nc, g * p).astype(BF16)
    xs_im = xs4[1].reshape(nc, g * p).astype(BF16)
    y = _s5_out_call(u3, bd, xs_re, xs_im, wo_c, rep, hh, p).reshape(n, d)
    return _s5_post_call(h, y, modr, ng, d_skip.reshape(1, d), w_glu, seq)


def _router_body(h_ref, modr_ref, ng_ref, whi_ref, wlo_ref, rb_ref, u_ref, meta_ref, cnt_ref,
                 carry, *, n_exp, tm):
    @pl.when(pl.program_id(0) == 0)
    def _():
        carry[...] = jnp.zeros_like(carry)

    m = modr_ref[0]
    u = _norm_mod(h_ref[...], ng_ref[...], m[3:4], m[4:5])
    u_ref[...] = u
    uh = u.astype(BF16)
    ul = (u - uh.astype(F32)).astype(BF16)
    whi = whi_ref[...]
    logits = (jnp.dot(uh, whi, preferred_element_type=F32)
              + jnp.dot(ul, whi, preferred_element_type=F32)
              + jnp.dot(uh, wlo_ref[...], preferred_element_type=F32)) + rb_ref[...]
    lane = lax.broadcasted_iota(jnp.int32, (tm, LANES), 1)
    lane_f = lane.astype(F32)
    l = jnp.where(lane < n_exp, logits, NEG_BIG)
    tops, idxs, sels = [], [], []
    for _ in range(TOP_K):
        mk = jnp.max(l, axis=-1, keepdims=True)
        ik = jnp.min(jnp.where(l == mk, lane_f, 1.0e9), axis=-1, keepdims=True)
        sk = lane_f == ik
        l = jnp.where(sk, NEG_BIG, l)
        tops.append(mk)
        idxs.append(ik)
        sels.append(sk)
    es = [jnp.exp(t - tops[0]) for t in tops]
    den = es[0]
    for e in es[1:]:
        den = den + e
    chosen = jnp.zeros((tm, LANES), F32)
    for sk in sels:
        chosen = jnp.where(sk, 1.0, chosen)
    row = lax.broadcasted_iota(jnp.int32, (tm, tm), 0)
    col = lax.broadcasted_iota(jnp.int32, (tm, tm), 1)
    tri = jnp.where(row > col, 1.0, 0.0).astype(BF16)
    rank_mat = jnp.dot(tri, chosen.astype(BF16), preferred_element_type=F32) + carry[0:1, :]
    meta = jnp.zeros((tm, LANES), F32)
    for k in range(TOP_K):
        rk = jnp.sum(jnp.where(sels[k], rank_mat, 0.0), axis=-1, keepdims=True)
        meta = jnp.where(lane == k, idxs[k], meta)
        meta = jnp.where(lane == TOP_K + k, rk, meta)
        meta = jnp.where(lane == 2 * TOP_K + k, es[k] / den, meta)
    meta_ref[...] = meta
    carry[...] = carry[...] + jnp.sum(chosen, axis=0, keepdims=True)
    cnt_ref[...] = carry[...]


def _router_call(h, modr, ng, w_router, b_router, seq):
    n, d = h.shape
    n_exp = w_router.shape[1]
    tm = min(ROW_TILE, seq)
    tpb = seq // tm
    wpad = jnp.zeros((d, LANES), F32).at[:, :n_exp].set(w_router)
    whi = wpad.astype(BF16)
    wlo = (wpad - whi.astype(F32)).astype(BF16)
    rb = jnp.zeros((1, LANES), F32).at[0, :n_exp].set(b_router)
    return pl.pallas_call(
        functools.partial(_router_body, n_exp=n_exp, tm=tm),
        out_shape=(
            jax.ShapeDtypeStruct((n, d), F32),
            jax.ShapeDtypeStruct((n, LANES), F32),
            jax.ShapeDtypeStruct((SUBLANES, LANES), F32),
        ),
        grid=(n // tm,),
        in_specs=[
            pl.BlockSpec((tm, d), lambda i: (i, 0)),
            pl.BlockSpec((1, SUBLANES, d), lambda i: (i // tpb, 0, 0)),
            _resident((1, d)),
            _resident(whi.shape),
            _resident(wlo.shape),
            _resident(rb.shape),
        ],
        out_specs=(
            pl.BlockSpec((tm, d), lambda i: (i, 0)),
            pl.BlockSpec((tm, LANES), lambda i: (i, 0)),
            pl.BlockSpec((SUBLANES, LANES), lambda i: (0, 0)),
        ),
        scratch_shapes=[pltpu.VMEM((SUBLANES, LANES), F32)],
        compiler_params=_cparams(("arbitrary",)),
        name="moe_router",
    )(h, modr, ng, whi, wlo, rb)


def _row_copy(src_ref, src_row, dst_ref, dst_row, sem):
    return pltpu.make_async_copy(src_ref.at[pl.ds(src_row, 1)], dst_ref.at[pl.ds(dst_row, 1)], sem)


def _rows_wait(hbm_ref, rows, sem):
    pltpu.make_async_copy(hbm_ref.at[pl.ds(0, rows)], hbm_ref.at[pl.ds(0, rows)], sem).wait()


def _dispatch_body(npad_ref, padrow_ref, dest_ref, u_ref, xs_ref, zrow, sem, *, td, pp):
    i = pl.program_id(0)
    zrow[...] = jnp.zeros_like(zrow)
    n_pad = jnp.clip(npad_ref[0] - i * pp, 0, pp)

    def issue(r, c):
        for k in range(TOP_K):
            _row_copy(u_ref, r, xs_ref, dest_ref[TOP_K * r + k], sem).start()
        return c

    lax.fori_loop(0, td, issue, 0, unroll=DMA_UNROLL)

    def issue_pad(j, c):
        _row_copy(zrow, 0, xs_ref, padrow_ref[i * pp + j], sem).start()
        return c

    lax.fori_loop(0, n_pad, issue_pad, 0)

    _rows_wait(xs_ref, TOP_K * td, sem)

    def wait_one(j, c):
        _row_copy(zrow, 0, xs_ref, 0, sem).wait()
        return c

    lax.fori_loop(0, n_pad, wait_one, 0)


def _dispatch_call(u, dest, pad_rows, n_pad, n_slots):
    n, d = u.shape
    td = min(ROW_TILE, n)
    steps = n // td
    pp = -(-pad_rows.shape[0] // steps)
    return pl.pallas_call(
        functools.partial(_dispatch_body, td=td, pp=pp),
        out_shape=jax.ShapeDtypeStruct((n_slots, d), F32),
        grid_spec=pltpu.PrefetchScalarGridSpec(
            num_scalar_prefetch=2,
            grid=(steps,),
            in_specs=[
                pl.BlockSpec((TOP_K * td,), lambda i, npad, prow: (i,), memory_space=pltpu.SMEM),
                pl.BlockSpec((td, d), lambda i, npad, prow: (i, 0)),
            ],
            out_specs=pl.BlockSpec(memory_space=pl.ANY),
            scratch_shapes=[pltpu.VMEM((SUBLANES, d), F32), pltpu.SemaphoreType.DMA(())],
        ),
        compiler_params=pltpu.CompilerParams(dimension_semantics=("arbitrary",),
                                             vmem_limit_bytes=VMEM_LIMIT, has_side_effects=True,
                                             disable_bounds_checks=True),
        name="moe_dispatch",
    )(n_pad, pad_rows, dest, u)


def _expert_body(bexp_ref, nused_ref, x_ref, wg_ref, bg_ref, wu_ref, bu_ref, wd_ref, bd_ref, o_ref,
                 wg16, wu16, wd16):
    @pl.when(pl.program_id(0) < nused_ref[0])
    def _():
        b = pl.program_id(0)
        prev = bexp_ref[jnp.maximum(b - 1, 0)]

        @pl.when((b == 0) | (bexp_ref[b] != prev))
        def _():
            wg16[...] = wg_ref[0, 0].astype(BF16)
            wu16[...] = wu_ref[0, 0].astype(BF16)
            wd16[...] = wd_ref[0, 0].astype(BF16)

        x = x_ref[...].astype(BF16)
        g = jnp.dot(x, wg16[...], preferred_element_type=F32) + bg_ref[0, 0]
        g = jnp.minimum(g, SWIGLU_LIMIT)
        v = jnp.dot(x, wu16[...], preferred_element_type=F32) + bu_ref[0, 0]
        v = jnp.clip(v, -SWIGLU_LIMIT, SWIGLU_LIMIT)
        a = g * jax.nn.sigmoid(SWIGLU_ALPHA * g) * (v + 1.0)
        o_ref[...] = jnp.dot(a.astype(BF16), wd16[...], preferred_element_type=F32) + bd_ref[0, 0]


def _expert_call(xs, block_exp, n_used, layer, wg, bg, wu, bu, wd, bd):
    n_slots, d = xs.shape
    depth, n_exp, _, f = wg.shape
    te = EXPERT_TILE
    nblk = n_slots // te

    def xmap(b, be, nu):
        return (jnp.minimum(b, nu[0] - 1), 0)

    def wmap(b, be, nu):
        return (layer, be[b], 0, 0)

    return pl.pallas_call(
        _expert_body,
        out_shape=jax.ShapeDtypeStruct((n_slots, d), F32),
        grid_spec=pltpu.PrefetchScalarGridSpec(
            num_scalar_prefetch=2,
            grid=(nblk,),
            in_specs=[
                pl.BlockSpec((te, d), xmap),
                pl.BlockSpec((1, 1, d, f), wmap),
                pl.BlockSpec((1, 1, 1, f), wmap),
                pl.BlockSpec((1, 1, d, f), wmap),
                pl.BlockSpec((1, 1, 1, f), wmap),
                pl.BlockSpec((1, 1, f, d), wmap),
                pl.BlockSpec((1, 1, 1, d), wmap),
            ],
            out_specs=pl.BlockSpec((te, d), xmap),
            scratch_shapes=[pltpu.VMEM((d, f), BF16), pltpu.VMEM((d, f), BF16),
                            pltpu.VMEM((f, d), BF16)],
        ),
        compiler_params=pltpu.CompilerParams(dimension_semantics=("arbitrary",),
                                             vmem_limit_bytes=EXPERT_VMEM_LIMIT),
        name="moe_experts",
    )(block_exp, n_used, xs, wg, bg.reshape(depth, n_exp, 1, f), wu, bu.reshape(depth, n_exp, 1, f),
      wd, bd.reshape(depth, n_exp, 1, d))


def _combine_body(dest_ref, dnext_ref, h_ref, meta_ref, modr_ref, fg_ref, ys_ref, o_ref, buf, sem,
                  *, tc, final):
    i = pl.program_id(0)
    slot = i % 2

    def gather(d_ref, s):
        def issue(r, c):
            for k in range(TOP_K):
                pltpu.make_async_copy(ys_ref.at[pl.ds(d_ref[TOP_K * r + k], 1)],
                                      buf.at[s, k, pl.ds(r, 1)], sem.at[s]).start()
            return c

        lax.fori_loop(0, tc, issue, 0, unroll=DMA_UNROLL)

    @pl.when(i == 0)
    def _():
        gather(dest_ref, 0)

    @pl.when(i + 1 < pl.num_programs(0))
    def _():
        gather(dnext_ref, 1 - slot)

    _rows_wait(ys_ref, TOP_K * tc, sem.at[slot])

    meta = meta_ref[...]
    acc = meta[:, 2 * TOP_K:2 * TOP_K + 1] * buf[slot, 0]
    for k in range(1, TOP_K):
        acc = acc + meta[:, 2 * TOP_K + k:2 * TOP_K + k + 1] * buf[slot, k]
    out = h_ref[...] + modr_ref[0][5:6] * acc
    if final:
        ms = jnp.mean(out * out, axis=-1, keepdims=True)
        out = out * lax.rsqrt(ms + NORM_EPS) * fg_ref[...]
    o_ref[...] = out


def _combine_call(h, meta, modr, final_g, ys, dest, seq, final):
    n, d = h.shape
    tc = min(ROW_TILE // 2, seq)
    tpb = seq // tc
    steps = n // tc
    return pl.pallas_call(
        functools.partial(_combine_body, tc=tc, final=final),
        out_shape=jax.ShapeDtypeStruct((n, d), F32),
        grid=(steps,),
        in_specs=[
            pl.BlockSpec((TOP_K * tc,), lambda i: (i,), memory_space=pltpu.SMEM),
            pl.BlockSpec((TOP_K * tc,), lambda i: (jnp.minimum(i + 1, steps - 1),),
                         memory_space=pltpu.SMEM),
            pl.BlockSpec((tc, d), lambda i: (i, 0)),
            pl.BlockSpec((tc, LANES), lambda i: (i, 0)),
            pl.BlockSpec((1, SUBLANES, d), lambda i: (i // tpb, 0, 0)),
            _resident((1, d)),
            pl.BlockSpec(memory_space=pl.ANY),
        ],
        out_specs=pl.BlockSpec((tc, d), lambda i: (i, 0)),
        scratch_shapes=[pltpu.VMEM((2, TOP_K, tc, d), F32), pltpu.SemaphoreType.DMA((2,))],
        compiler_params=pltpu.CompilerParams(dimension_semantics=("arbitrary",),
                                             vmem_limit_bytes=VMEM_LIMIT,
                                             disable_bounds_checks=True),
        name="moe_combine",
    )(dest, dest, h, meta, modr, final_g, ys)


def _moe(h, modr, ng, w_router, b_router, layer, wg, bg, wu, bu, wd, bd, final_g, seq, final):
    n, d = h.shape
    n_exp = w_router.shape[1]
    te = EXPERT_TILE
    n_slots = n * TOP_K + n_exp * te
    nblk = n_slots // te
    u, meta, cnt = _router_call(h, modr, ng, w_router, b_router, seq)

    idx = meta[:, 0:TOP_K].astype(jnp.int32)
    rank = meta[:, TOP_K:2 * TOP_K].astype(jnp.int32)
    counts = cnt[0, :n_exp].astype(jnp.int32)
    padded = (counts + te - 1) // te * te
    pad_ends = jnp.cumsum(padded)
    pad_starts = pad_ends - padded
    dest = (pad_starts[idx] + rank).reshape(-1)
    n_used = pad_ends[-1] // te
    blk = jnp.minimum(jnp.arange(nblk, dtype=jnp.int32), n_used - 1)
    block_exp = jnp.minimum(jnp.sum(pad_ends[None, :] <= (blk * te)[:, None], axis=1),
                            n_exp - 1).astype(jnp.int32)
    within = jnp.arange(te, dtype=jnp.int32)[None, :]
    n_fill = (padded - counts)[:, None]
    fill_rows = jnp.where(within < n_fill, (pad_starts + counts)[:, None] + within, n_slots)
    pad_rows = jnp.sort(fill_rows.reshape(-1)).astype(jnp.int32)
    n_pad = jnp.sum(padded - counts).astype(jnp.int32).reshape(1)

    xs = _dispatch_call(u, dest, pad_rows, n_pad, n_slots)
    ys = _expert_call(xs, block_exp, n_used.astype(jnp.int32).reshape(1), layer,
                      wg, bg, wu, bu, wd, bd)
    return _combine_call(h, meta, modr, final_g, ys, dest, seq, final)


def kernel(x, c, ada_w, ada_b, norm_mix_g, norm_ffn_g, cv_w_in, cv_sc_w, cv_cf_w, cv_cf_b, cv_ln_g, cv_ln_b, cv_w_out, s5_lam_re, s5_lam_im, s5_log_dt, s5_b_re, s5_b_im, s5_c_re, s5_c_im, s5_d, s5_w_glu, router_w, router_b, exp_w_gate, exp_b_gate, exp_w_up, exp_b_up, exp_w_down, exp_b_down, final_g):
    bsz, seq, d = x.shape
    depth = ada_w.shape[0]
    n = bsz * seq
    assert bsz <= SUBLANES
    h = x.reshape(n, d)
    mod = _mod_call(c, ada_w, ada_b)
    fg = final_g.reshape(1, d)
    for layer in range(depth):
        modr = jnp.zeros((bsz, SUBLANES, d), F32).at[:, :6].set(mod[layer, :bsz].reshape(bsz, 6, d))
        ng = norm_mix_g[layer].reshape(1, d)
        j = layer // 2
        if layer % 2 == 0:
            h = _conv_mixer_call(h, modr, ng, cv_w_in[j], cv_sc_w[j], cv_cf_w[j], cv_cf_b[j],
                                 cv_ln_g[j], cv_ln_b[j], cv_w_out[j], bsz, seq)
        else:
            h = _s5_mixer(h, modr, ng, s5_lam_re[j], s5_lam_im[j], s5_log_dt[j], s5_b_re[j],
                          s5_b_im[j], s5_c_re[j], s5_c_im[j], s5_d[j], s5_w_glu[j], bsz, seq)
        h = _moe(h, modr, norm_ffn_g[layer].reshape(1, d), router_w[layer], router_b[layer],
                 layer, exp_w_gate, exp_b_gate, exp_w_up, exp_b_up, exp_w_down, exp_b_down, fg, seq,
                 layer == depth - 1)
    return h.reshape(bsz, seq, d)
```

```python
import functools

import jax
import jax.numpy as jnp
from jax import lax
from jax.experimental import pallas as pl
from jax.experimental.pallas import tpu as pltpu

F32 = jnp.float32
BF16 = jnp.bfloat16
NORM_EPS = 1e-5
TOP_K = 4
SWIGLU_LIMIT = 7.0
SWIGLU_ALPHA = 1.702
LANES = 128
SUBLANES = 8
VMEM_LIMIT = 56 * 1024 * 1024
EXPERT_VMEM_LIMIT = 60 * 1024 * 1024
S5_CHUNK = 16
ROW_TILE = 256
EXPERT_TILE = 256
DMA_UNROLL = 8
NEG_BIG = -3.0e38


def _cparams(sem):
    return pltpu.CompilerParams(dimension_semantics=sem, vmem_limit_bytes=VMEM_LIMIT)


def _largest_divisor(n, candidates):
    for c in candidates:
        if n % c == 0:
            return c
    return n


def _norm_mod(x, g, shift, scale):
    ms = jnp.mean(x * x, axis=-1, keepdims=True)
    return x * lax.rsqrt(ms + NORM_EPS) * g * (1.0 + scale) + shift


def _resident(shape):
    nd = len(shape)
    return pl.BlockSpec(shape, lambda *_: (0,) * nd, pipeline_mode=pl.Buffered(1))


def _mod_body(c_ref, w_ref, b_ref, o_ref):
    c = c_ref[...]
    ca = (c * jax.nn.sigmoid(c)).astype(BF16)
    o_ref[0] = jnp.dot(ca, w_ref[0].astype(BF16), preferred_element_type=F32) + b_ref[0]


def _mod_call(c, ada_w, ada_b):
    depth, d, w = ada_w.shape
    b = c.shape[0]
    c8 = jnp.zeros((SUBLANES, d), F32).at[:b].set(c)
    tn = _largest_divisor(w, (1536, 1024, 768, 512, 384, 256, 128))
    return pl.pallas_call(
        _mod_body,
        out_shape=jax.ShapeDtypeStruct((depth, SUBLANES, w), F32),
        grid=(depth, w // tn),
        in_specs=[
            pl.BlockSpec((SUBLANES, d), lambda l, j: (0, 0)),
            pl.BlockSpec((1, d, tn), lambda l, j: (l, 0, j)),
            pl.BlockSpec((1, 1, tn), lambda l, j: (l, 0, j)),
        ],
        out_specs=pl.BlockSpec((1, SUBLANES, tn), lambda l, j: (l, 0, j)),
        compiler_params=_cparams(("arbitrary", "arbitrary")),
        name="adaln_mod",
    )(c8, ada_w, ada_b.reshape(depth, 1, w))


def _conv_body(h_ref, modr_ref, ng_ref, win_ref, scw_ref, cfw_ref, cfv_ref, wout_ref, o_ref,
               gbuf, cvbuf, ycat, wbc, *, tm, sc, cf, ka, kb, ha, hb, rc):
    @pl.when((pl.program_id(0) == 0) & (pl.program_id(1) == 0))
    def _():
        for k in range(kb):
            wbc[k] = jnp.broadcast_to(cfw_ref[k:k + 1, :], (SUBLANES, cf))

    @pl.when(pl.program_id(1) == 0)
    def _():
        gbuf[0, 0:hb, :] = jnp.zeros((hb, cf), F32)
        cvbuf[0:ha, :] = jnp.zeros((ha, sc), F32)

    x = h_ref[...]
    m = modr_ref[0]
    u16 = _norm_mod(x, ng_ref[...], m[0:1], m[1:2]).astype(BF16)

    zb = jnp.dot(u16, win_ref[:, 3 * sc:3 * sc + 2 * cf], preferred_element_type=F32)
    gbuf[0, hb:hb + tm, :] = zb[:, 0:cf] * jax.nn.sigmoid(zb[:, cf:2 * cf])
    span = tm + hb - SUBLANES
    for b in range(1, SUBLANES):
        for c0 in range(0, span, 64):
            c1 = min(c0 + 64, span)
            gbuf[b, c0:c1, :] = gbuf[0, b + c0:b + c1, :]

    za = jnp.dot(u16, win_ref[:, 0:3 * sc], preferred_element_type=F32)
    cv = za[:, sc:2 * sc] * za[:, 2 * sc:3 * sc]
    cvbuf[ha:ha + tm, :] = cv
    acc = scw_ref[ka - 1:ka, :] * cv
    for k in range(ka - 1):
        off = ha - (ka - 1) + k
        acc = acc + scw_ref[k:k + 1, :] * cvbuf[off:off + tm, :]
    ycat[:, 0:sc] = (za[:, 0:sc] * acc).astype(BF16)
    y = jnp.dot(ycat[:, 0:sc], wout_ref[0:sc, :], preferred_element_type=F32)

    cf_b = cfv_ref[0:1, :]
    ln_g = cfv_ref[1:2, :]
    ln_b = cfv_ref[2:3, :]
    for r0 in range(0, tm, rc):
        rows = range(r0, r0 + rc, SUBLANES)
        w_last = wbc[kb - 1]
        accs = [w_last * gbuf[0, hb + r:hb + r + SUBLANES, :] for r in rows]
        for k in range(kb - 1):
            off = hb - (kb - 1) + k
            al = off // SUBLANES * SUBLANES
            wk = wbc[k]
            for c, r in enumerate(rows):
                accs[c] = accs[c] + wk * gbuf[off % SUBLANES, al + r:al + r + SUBLANES, :]
        a = jnp.concatenate(accs, axis=0) + cf_b
        mu = jnp.mean(a, axis=-1, keepdims=True)
        ac = a - mu
        var = jnp.mean(ac * ac, axis=-1, keepdims=True)
        yn = ac * lax.rsqrt(var + NORM_EPS) * ln_g + ln_b
        ycat[r0:r0 + rc, sc:sc + cf] = (yn * jax.nn.sigmoid(yn)).astype(BF16)

    cvbuf[0:ha, :] = cvbuf[tm:tm + ha, :]
    gbuf[0, 0:hb, :] = gbuf[0, tm:tm + hb, :]

    y = y + jnp.dot(ycat[:, sc:sc + cf], wout_ref[sc:sc + cf, :], preferred_element_type=F32)
    o_ref[...] = x + m[2:3] * y


def _conv_mixer_call(h, modr, ng, w_in, sc_w, cf_w, cf_b, ln_g, ln_b, w_out, bsz, seq):
    n, d = h.shape
    ka, sc = sc_w.shape
    kb, cf = cf_w.shape
    ha = -(-(ka - 1) // SUBLANES) * SUBLANES
    hb = -(-(kb - 1) // SUBLANES) * SUBLANES
    tm = min(ROW_TILE, seq)
    assert seq % tm == 0 and tm >= hb and tm >= ha
    ns = seq // tm
    scw = jnp.zeros((SUBLANES, sc), F32).at[:ka].set(sc_w)
    cfw = jnp.zeros((-(-kb // SUBLANES) * SUBLANES, cf), F32).at[:kb].set(cf_w)
    cfv = jnp.zeros((SUBLANES, cf), F32).at[0].set(cf_b).at[1].set(ln_g).at[2].set(ln_b)
    body = functools.partial(_conv_body, tm=tm, sc=sc, cf=cf, ka=ka, kb=kb, ha=ha, hb=hb,
                             rc=min(16, tm))
    return pl.pallas_call(
        body,
        out_shape=jax.ShapeDtypeStruct((n, d), F32),
        grid=(bsz, ns),
        in_specs=[
            pl.BlockSpec((tm, d), lambda b, s: (b * ns + s, 0)),
            pl.BlockSpec((1, SUBLANES, d), lambda b, s: (b, 0, 0)),
            _resident((1, d)),
            _resident(w_in.shape),
            _resident(scw.shape),
            _resident(cfw.shape),
            _resident(cfv.shape),
            _resident(w_out.shape),
        ],
        out_specs=pl.BlockSpec((tm, d), lambda b, s: (b * ns + s, 0)),
        scratch_shapes=[
            pltpu.VMEM((SUBLANES, tm + hb, cf), F32),
            pltpu.VMEM((tm + ha, sc), F32),
            pltpu.VMEM((tm, sc + cf), BF16),
            pltpu.VMEM((cfw.shape[0], SUBLANES, cf), F32),
        ],
        compiler_params=_cparams(("arbitrary", "arbitrary")),
        name="conv_mixer",
    )(h, modr, ng, w_in.astype(BF16), scw, cfw, cfv, w_out.astype(BF16))


def _norm_body(h_ref, modr_ref, ng_ref, o_ref):
    m = modr_ref[0]
    o_ref[...] = _norm_mod(h_ref[...], ng_ref[...], m[0:1], m[1:2]).astype(BF16)


def _norm_call(h, modr, ng, seq):
    n, d = h.shape
    tm = min(2 * ROW_TILE, seq)
    tpb = seq // tm
    return pl.pallas_call(
        _norm_body,
        out_shape=jax.ShapeDtypeStruct((n, d), BF16),
        grid=(n // tm,),
        in_specs=[
            pl.BlockSpec((tm, d), lambda i: (i, 0)),
            pl.BlockSpec((1, SUBLANES, d), lambda i: (i // tpb, 0, 0)),
            _resident((1, d)),
        ],
        out_specs=pl.BlockSpec((tm, d), lambda i: (i, 0)),
        compiler_params=_cparams(("arbitrary",)),
        name="s5_norm",
    )(h, modr, ng)


def _s5_prep(lam_re, lam_im, log_dt, b_re, b_im, c_re, c_im, chunk):
    hi = lax.Precision.HIGHEST
    dt = jnp.exp(log_dt)[:, None]
    mag = jnp.exp(lam_re * dt)
    lb_re = mag * jnp.cos(lam_im * dt)
    lb_im = mag * jnp.sin(lam_im * dt)
    den = lam_re * lam_re + lam_im * lam_im
    f_re = ((lb_re - 1) * lam_re + lb_im * lam_im) / den
    f_im = (lb_im * lam_re - (lb_re - 1) * lam_im) / den
    bb_re = f_re[..., None] * b_re - f_im[..., None] * b_im
    bb_im = f_re[..., None] * b_im + f_im[..., None] * b_re
    g, p, hh = bb_re.shape
    prs, pis = [jnp.ones_like(lb_re)], [jnp.zeros_like(lb_im)]
    for _ in range(chunk):
        pr, pi = prs[-1], pis[-1]
        prs.append(pr * lb_re - pi * lb_im)
        pis.append(pr * lb_im + pi * lb_re)
    pr = jnp.stack(prs)
    pi = jnp.stack(pis)
    kmat = (jnp.einsum('ghp,tgp,gpk->tghk', c_re, pr[:chunk], bb_re, precision=hi)
            - jnp.einsum('ghp,tgp,gpk->tghk', c_re, pi[:chunk], bb_im, precision=hi)
            - jnp.einsum('ghp,tgp,gpk->tghk', c_im, pr[:chunk], bb_im, precision=hi)
            - jnp.einsum('ghp,tgp,gpk->tghk', c_im, pi[:chunk], bb_re, precision=hi))
    prr = pr[chunk - 1::-1][:chunk]
    pir = pi[chunk - 1::-1][:chunk]
    w_re = prr[..., None] * bb_re[None] - pir[..., None] * bb_im[None]
    w_im = prr[..., None] * bb_im[None] + pir[..., None] * bb_re[None]
    w_in = jnp.concatenate([w_re.transpose(1, 0, 3, 2), w_im.transpose(1, 0, 3, 2)], axis=-1)
    w_in = w_in.reshape(g, chunk * hh, 2 * p)
    p1r = pr[1:chunk + 1][:, :, None, :]
    p1i = pi[1:chunk + 1][:, :, None, :]
    co_r = c_re[None] * p1r - c_im[None] * p1i
    co_i = -c_re[None] * p1i - c_im[None] * p1r
    w_out = jnp.concatenate([co_r.transpose(1, 3, 0, 2), co_i.transpose(1, 3, 0, 2)], axis=1)
    w_out = w_out.reshape(g, 2 * p, chunk * hh)
    return kmat, w_in, w_out, pr[chunk], pi[chunk]


def _s5_compact_blocks(kmat, w_in, w_out, chunk, hh, p):
    g = kmat.shape[1]
    gpb = LANES // hh
    nlb = g // gpb
    eye = jnp.eye(gpb, dtype=F32)
    k6 = kmat.reshape(chunk, nlb, gpb, hh, hh).transpose(1, 0, 2, 4, 3)
    bd = (k6[:, :, :, :, None, :] * eye[None, None, :, None, :, None]).reshape(
        nlb, chunk, LANES, LANES)
    wi = w_in.reshape(nlb, gpb, chunk, hh, 2, p).transpose(0, 2, 4, 1, 3, 5)
    wi = wi.reshape(nlb, chunk, 2, LANES, p)
    wi_c = jnp.concatenate([wi] * (LANES // p), axis=-1)
    wo_c = w_out.reshape(nlb, gpb, 2, p, chunk * hh).transpose(0, 2, 1, 3, 4)
    t_i = jnp.arange(chunk)
    h_i = jnp.arange(hh)
    rep = ((t_i[:, None, None, None, None] == t_i[None, None, :, None, None])
           & (h_i[None, :, None, None, None] == h_i[None, None, None, None, :]))
    rep = jnp.broadcast_to(rep, (chunk, hh, chunk, gpb, hh)).reshape(chunk * hh, chunk * LANES)
    return bd.astype(BF16), wi_c.astype(BF16), wo_c.astype(BF16), rep.astype(BF16)


def _fold_chunks(u_ref, ubuf, ucat, chunk):
    ubuf[...] = u_ref[...].astype(F32)
    for s in range(chunk):
        ucat[:, s * LANES:(s + 1) * LANES] = ubuf[:, s, :].astype(BF16)


def _s5_state_body(u_ref, wic_ref, ore_ref, oim_ref, uc_ref, ubuf, ucat, wvm, *, chunk, half, hh, p):
    @pl.when(pl.program_id(1) == 0)
    def _():
        row_g = lax.broadcasted_iota(jnp.int32, (LANES, LANES), 0) // hh
        lane_g = lax.broadcasted_iota(jnp.int32, (LANES, LANES), 1) // p
        for c in range(half // LANES):
            mask = row_g == (LANES // p) * c + lane_g
            for s in range(chunk):
                for q in range(2):
                    lo = q * half + c * LANES
                    wvm[s * LANES:(s + 1) * LANES, lo:lo + LANES] = jnp.where(
                        mask, wic_ref[0, s, q], jnp.zeros((LANES, LANES), BF16))

    _fold_chunks(u_ref, ubuf, ucat, chunk)
    uc_ref[0] = ucat[...]
    s = jnp.dot(ucat[...], wvm[...], preferred_element_type=F32)
    ore_ref[...] = s[:, 0:half]
    oim_ref[...] = s[:, half:2 * half]


def _s5_state_call(u3, wi_c, hh, p):
    nc, chunk, d = u3.shape
    nlb = d // LANES
    half = LANES // hh * p
    gp = nlb * half
    wide = chunk * LANES
    cb = _largest_divisor(nc, (256, 128, 64, 32, 16, 8))
    return pl.pallas_call(
        functools.partial(_s5_state_body, chunk=chunk, half=half, hh=hh, p=p),
        out_shape=(jax.ShapeDtypeStruct((nc, gp), F32), jax.ShapeDtypeStruct((nc, gp), F32),
                   jax.ShapeDtypeStruct((nlb, nc, wide), BF16)),
        grid=(nlb, nc // cb),
        in_specs=[
            pl.BlockSpec((cb, chunk, LANES), lambda j, i: (i, 0, j)),
            pl.BlockSpec((1, chunk, 2, LANES, LANES), lambda j, i: (j, 0, 0, 0, 0)),
        ],
        out_specs=(pl.BlockSpec((cb, half), lambda j, i: (i, j)),
                   pl.BlockSpec((cb, half), lambda j, i: (i, j)),
                   pl.BlockSpec((1, cb, wide), lambda j, i: (j, i, 0))),
        scratch_shapes=[pltpu.VMEM((cb, chunk, LANES), F32), pltpu.VMEM((cb, chunk * LANES), BF16),
                        pltpu.VMEM((chunk * LANES, 2 * half), BF16)],
        compiler_params=_cparams(("arbitrary", "arbitrary")),
        name="s5_chunk_state",
    )(u3, wi_c)


def _s5_scan_body(s_ref, a_ref, o_ref, carry, *, cb):
    @pl.when(pl.program_id(1) == 0)
    def _():
        carry[...] = jnp.zeros_like(carry)

    ar = a_ref[0]
    ai = a_ref[1]

    def step(c, st):
        xr, xi = st
        o_ref[0, c] = xr
        o_ref[1, c] = xi
        return (ar * xr - ai * xi + s_ref[0, c], ar * xi + ai * xr + s_ref[1, c])

    xr, xi = lax.fori_loop(0, cb, step, (carry[0], carry[1]))
    carry[0] = xr
    carry[1] = xi


def _s5_scan_call(s4, a2, bsz):
    _, nc, r, _ = s4.shape
    ncb = nc // bsz
    cb = _largest_divisor(ncb, (64, 32, 16, 8, 4, 2, 1))
    nb = ncb // cb
    return pl.pallas_call(
        functools.partial(_s5_scan_body, cb=cb),
        out_shape=jax.ShapeDtypeStruct(s4.shape, F32),
        grid=(bsz, nb),
        in_specs=[
            pl.BlockSpec((2, cb, r, LANES), lambda b, j: (0, b * nb + j, 0, 0)),
            _resident(a2.shape),
        ],
        out_specs=pl.BlockSpec((2, cb, r, LANES), lambda b, j: (0, b * nb + j, 0, 0)),
        scratch_shapes=[pltpu.VMEM((2, r, LANES), F32)],
        compiler_params=_cparams(("arbitrary", "arbitrary")),
        name="s5_state_scan",
    )(s4, a2)


def _s5_out_body(uc_ref, bd_ref, xre_ref, xim_ref, woc_ref, rep_ref, o_ref, ybuf, tvm, wovm,
                 *, chunk, half, hh, p):
    @pl.when(pl.program_id(1) == 0)
    def _():
        zero = jnp.zeros((LANES, LANES), BF16)
        for s in range(chunk):
            for t in range(chunk):
                tvm[s * LANES:(s + 1) * LANES, t * LANES:(t + 1) * LANES] = (
                    bd_ref[0, t - s] if t >= s else zero)
        lane_g = lax.broadcasted_iota(jnp.int32, (p, chunk * LANES), 1) % LANES // hh
        gpb = LANES // hh
        for q in range(2):
            for a in range(gpb):
                spread = jnp.dot(woc_ref[0, q, a], rep_ref[...], preferred_element_type=F32)
                wovm[(q * gpb + a) * p:(q * gpb + a + 1) * p, :] = jnp.where(
                    lane_g == a, spread, 0.0).astype(BF16)

    y = jnp.dot(uc_ref[0], tvm[...], preferred_element_type=F32)
    y = y + jnp.dot(xre_ref[...], wovm[0:half, :], preferred_element_type=F32)
    y = y + jnp.dot(xim_ref[...], wovm[half:2 * half, :], preferred_element_type=F32)
    for t in range(chunk):
        ybuf[:, t, :] = y[:, t * LANES:(t + 1) * LANES]
    o_ref[...] = ybuf[...].astype(BF16)


def _s5_out_call(ucat, bd, xs_re, xs_im, wo_c, rep, hh, p):
    nlb, nc, wide = ucat.shape
    chunk = wide // LANES
    d = nlb * LANES
    gpb = LANES // hh
    half = gpb * p
    cb = _largest_divisor(nc, (256, 128, 64, 32, 16, 8))
    return pl.pallas_call(
        functools.partial(_s5_out_body, chunk=chunk, half=half, hh=hh, p=p),
        out_shape=jax.ShapeDtypeStruct((nc, chunk, d), BF16),
        grid=(nlb, nc // cb),
        in_specs=[
            pl.BlockSpec((1, cb, wide), lambda j, i: (j, i, 0)),
            pl.BlockSpec((1, chunk, LANES, LANES), lambda j, i: (j, 0, 0, 0)),
            pl.BlockSpec((cb, half), lambda j, i: (i, j)),
            pl.BlockSpec((cb, half), lambda j, i: (i, j)),
            pl.BlockSpec((1, 2, gpb, p, chunk * hh), lambda j, i: (j, 0, 0, 0, 0)),
            _resident(rep.shape),
        ],
        out_specs=pl.BlockSpec((cb, chunk, LANES), lambda j, i: (i, 0, j)),
        scratch_shapes=[pltpu.VMEM((cb, chunk, LANES), F32), pltpu.VMEM((wide, wide), BF16),
                        pltpu.VMEM((2 * half, wide), BF16)],
        compiler_params=_cparams(("arbitrary", "arbitrary")),
        name="s5_chunk_out",
    )(ucat, bd, xs_re, xs_im, wo_c, rep)


def _s5_post_body(h_ref, y_ref, modr_ref, ng_ref, dsk_ref, wglu_ref, o_ref, *, d):
    x = h_ref[...]
    m = modr_ref[0]
    u = _norm_mod(x, ng_ref[...], m[0:1], m[1:2])
    yy = jax.nn.gelu(y_ref[...].astype(F32) + dsk_ref[...] * u)
    r = jnp.dot(yy.astype(BF16), wglu_ref[...], preferred_element_type=F32)
    o_ref[...] = x + m[2:3] * (r[:, 0:d] * jax.nn.sigmoid(r[:, d:2 * d]))


def _s5_post_call(h, y, modr, ng, d_skip, w_glu, seq):
    n, d = h.shape
    tm = min(ROW_TILE, seq)
    tpb = seq // tm
    return pl.pallas_call(
        functools.partial(_s5_post_body, d=d),
        out_shape=jax.ShapeDtypeStruct((n, d), F32),
        grid=(n // tm,),
        in_specs=[
            pl.BlockSpec((tm, d), lambda i: (i, 0)),
            pl.BlockSpec((tm, d), lambda i: (i, 0)),
            pl.BlockSpec((1, SUBLANES, d), lambda i: (i // tpb, 0, 0)),
            _resident((1, d)),
            _resident((1, d)),
            _resident(w_glu.shape),
        ],
        out_specs=pl.BlockSpec((tm, d), lambda i: (i, 0)),
        compiler_params=_cparams(("arbitrary",)),
        name="s5_glu",
    )(h, y, modr, ng, d_skip, w_glu.astype(BF16))


def _s5_mixer(h, modr, ng, lam_re, lam_im, log_dt, b_re, b_im, c_re, c_im, d_skip, w_glu,
              bsz, seq):
    n, d = h.shape
    g, p, hh = b_re.shape
    chunk = min(S5_CHUNK, seq)
    nc = n // chunk
    kmat, w_in, w_out, a_re, a_im = _s5_prep(lam_re, lam_im, log_dt, b_re, b_im, c_re, c_im, chunk)
    bd, wi_c, wo_c, rep = _s5_compact_blocks(kmat, w_in, w_out, chunk, hh, p)
    u3 = _norm_call(h, modr, ng, seq).reshape(nc, chunk, d)
    s_re, s_im, ucat = _s5_state_call(u3, wi_c, hh, p)
    r = g * p // LANES
    s4 = jnp.stack([s_re, s_im]).reshape(2, nc, r, LANES)
    a2 = jnp.stack([a_re, a_im]).reshape(2, r, LANES)
    xs4 = _s5_scan_call(s4, a2, bsz)
    xs_re = xs4[0].reshape(nc, g * p).astype(BF16)
    xs_im = xs4[1].reshape(nc, g * p).astype(BF16)
    y = _s5_out_call(ucat, bd, xs_re, xs_im, wo_c, rep, hh, p).reshape(n, d)
    return _s5_post_call(h, y, modr, ng, d_skip.reshape(1, d), w_glu, seq)


U32 = jnp.uint32
HI_MASK = 0xFFFF0000


def _pack_pairs(x):
    half = x.shape[1] // 2
    lo = lax.bitcast_convert_type(x[:, :half].astype(BF16).astype(F32), U32)
    hi = lax.bitcast_convert_type(x[:, half:].astype(BF16).astype(F32), U32)
    return (lo >> 16) | (hi & U32(HI_MASK))


def _unpack_pairs(w):
    lo = lax.bitcast_convert_type(w << 16, F32)
    hi = lax.bitcast_convert_type(w & U32(HI_MASK), F32)
    return lo, hi


def _router_body(h_ref, modr_ref, ng_ref, whi_ref, wlo_ref, rb_ref, u_ref, meta_ref, cnt_ref,
                 carry, *, n_exp, tm):
    @pl.when(pl.program_id(0) == 0)
    def _():
        carry[...] = jnp.zeros_like(carry)

    m = modr_ref[0]
    u = _norm_mod(h_ref[...], ng_ref[...], m[3:4], m[4:5])
    u_ref[...] = _pack_pairs(u)
    uh = u.astype(BF16)
    ul = (u - uh.astype(F32)).astype(BF16)
    whi = whi_ref[...]
    logits = (jnp.dot(uh, whi, preferred_element_type=F32)
              + jnp.dot(ul, whi, preferred_element_type=F32)
              + jnp.dot(uh, wlo_ref[...], preferred_element_type=F32)) + rb_ref[...]
    lane = lax.broadcasted_iota(jnp.int32, (tm, LANES), 1)
    lane_f = lane.astype(F32)
    l = jnp.where(lane < n_exp, logits, NEG_BIG)
    tops, idxs, sels = [], [], []
    for _ in range(TOP_K):
        mk = jnp.max(l, axis=-1, keepdims=True)
        ik = jnp.min(jnp.where(l == mk, lane_f, 1.0e9), axis=-1, keepdims=True)
        sk = lane_f == ik
        l = jnp.where(sk, NEG_BIG, l)
        tops.append(mk)
        idxs.append(ik)
        sels.append(sk)
    es = [jnp.exp(t - tops[0]) for t in tops]
    den = es[0]
    for e in es[1:]:
        den = den + e
    chosen = jnp.zeros((tm, LANES), F32)
    for sk in sels:
        chosen = jnp.where(sk, 1.0, chosen)
    row = lax.broadcasted_iota(jnp.int32, (tm, tm), 0)
    col = lax.broadcasted_iota(jnp.int32, (tm, tm), 1)
    tri = jnp.where(row > col, 1.0, 0.0).astype(BF16)
    rank_mat = jnp.dot(tri, chosen.astype(BF16), preferred_element_type=F32) + carry[0:1, :]
    meta = jnp.zeros((tm, LANES), F32)
    for k in range(TOP_K):
        rk = jnp.sum(jnp.where(sels[k], rank_mat, 0.0), axis=-1, keepdims=True)
        meta = jnp.where(lane == k, idxs[k], meta)
        meta = jnp.where(lane == TOP_K + k, rk, meta)
        meta = jnp.where(lane == 2 * TOP_K + k, es[k] / den, meta)
    meta_ref[...] = meta
    carry[...] = carry[...] + jnp.sum(chosen, axis=0, keepdims=True)
    cnt_ref[...] = carry[...]


def _router_call(h, modr, ng, w_router, b_router, seq):
    n, d = h.shape
    n_exp = w_router.shape[1]
    tm = min(ROW_TILE, seq)
    tpb = seq // tm
    wpad = jnp.zeros((d, LANES), F32).at[:, :n_exp].set(w_router)
    whi = wpad.astype(BF16)
    wlo = (wpad - whi.astype(F32)).astype(BF16)
    rb = jnp.zeros((1, LANES), F32).at[0, :n_exp].set(b_router)
    return pl.pallas_call(
        functools.partial(_router_body, n_exp=n_exp, tm=tm),
        out_shape=(
            jax.ShapeDtypeStruct((n, d // 2), U32),
            jax.ShapeDtypeStruct((n, LANES), F32),
            jax.ShapeDtypeStruct((SUBLANES, LANES), F32),
        ),
        grid=(n // tm,),
        in_specs=[
            pl.BlockSpec((tm, d), lambda i: (i, 0)),
            pl.BlockSpec((1, SUBLANES, d), lambda i: (i // tpb, 0, 0)),
            _resident((1, d)),
            _resident(whi.shape),
            _resident(wlo.shape),
            _resident(rb.shape),
        ],
        out_specs=(
            pl.BlockSpec((tm, d // 2), lambda i: (i, 0)),
            pl.BlockSpec((tm, LANES), lambda i: (i, 0)),
            pl.BlockSpec((SUBLANES, LANES), lambda i: (0, 0)),
        ),
        scratch_shapes=[pltpu.VMEM((SUBLANES, LANES), F32)],
        compiler_params=_cparams(("arbitrary",)),
        name="moe_router",
    )(h, modr, ng, whi, wlo, rb)


def _row_copy(src_ref, src_row, dst_ref, dst_row, sem):
    return pltpu.make_async_copy(src_ref.at[pl.ds(src_row, 1)], dst_ref.at[pl.ds(dst_row, 1)], sem)


def _rows_wait(hbm_ref, rows, sem):
    pltpu.make_async_copy(hbm_ref.at[pl.ds(0, rows)], hbm_ref.at[pl.ds(0, rows)], sem).wait()


def _dispatch_body(npad_ref, padrow_ref, dest_ref, u_ref, xs_ref, zrow, sem, *, td, pp):
    i = pl.program_id(0)
    zrow[...] = jnp.zeros_like(zrow)
    n_pad = jnp.clip(npad_ref[0] - i * pp, 0, pp)

    def issue(r, c):
        for k in range(TOP_K):
            _row_copy(u_ref, r, xs_ref, dest_ref[TOP_K * r + k], sem).start()
        return c

    lax.fori_loop(0, td, issue, 0, unroll=DMA_UNROLL)

    def issue_pad(j, c):
        _row_copy(zrow, 0, xs_ref, padrow_ref[i * pp + j], sem).start()
        return c

    lax.fori_loop(0, n_pad, issue_pad, 0)

    _rows_wait(xs_ref, TOP_K * td, sem)

    def wait_one(j, c):
        _row_copy(zrow, 0, xs_ref, 0, sem).wait()
        return c

    lax.fori_loop(0, n_pad, wait_one, 0)


def _dispatch_call(u, dest, pad_rows, n_pad, n_slots):
    n, d = u.shape
    td = min(ROW_TILE, n)
    steps = n // td
    pp = -(-pad_rows.shape[0] // steps)
    return pl.pallas_call(
        functools.partial(_dispatch_body, td=td, pp=pp),
        out_shape=jax.ShapeDtypeStruct((n_slots, d), u.dtype),
        grid_spec=pltpu.PrefetchScalarGridSpec(
            num_scalar_prefetch=2,
            grid=(steps,),
            in_specs=[
                pl.BlockSpec((TOP_K * td,), lambda i, npad, prow: (i,), memory_space=pltpu.SMEM),
                pl.BlockSpec((td, d), lambda i, npad, prow: (i, 0)),
            ],
            out_specs=pl.BlockSpec(memory_space=pl.ANY),
            scratch_shapes=[pltpu.VMEM((SUBLANES, d), u.dtype), pltpu.SemaphoreType.DMA(())],
        ),
        compiler_params=pltpu.CompilerParams(dimension_semantics=("arbitrary",),
                                             vmem_limit_bytes=VMEM_LIMIT, has_side_effects=True,
                                             disable_bounds_checks=True),
        name="moe_dispatch",
    )(n_pad, pad_rows, dest, u)


def _expert_body(bexp_ref, nused_ref, x_ref, wg_ref, bg_ref, wu_ref, bu_ref, wd_ref, bd_ref, o_ref,
                 wg16, wu16, wd16):
    @pl.when(pl.program_id(0) < nused_ref[0])
    def _():
        b = pl.program_id(0)
        prev = bexp_ref[jnp.maximum(b - 1, 0)]

        @pl.when((b == 0) | (bexp_ref[b] != prev))
        def _():
            wg16[...] = wg_ref[0, 0].astype(BF16)
            wu16[...] = wu_ref[0, 0].astype(BF16)
            wd16[...] = wd_ref[0, 0].astype(BF16)

        x_lo, x_hi = _unpack_pairs(x_ref[...])
        x = jnp.concatenate([x_lo.astype(BF16), x_hi.astype(BF16)], axis=1)
        g = jnp.dot(x, wg16[...], preferred_element_type=F32) + bg_ref[0, 0]
        g = jnp.minimum(g, SWIGLU_LIMIT)
        v = jnp.dot(x, wu16[...], preferred_element_type=F32) + bu_ref[0, 0]
        v = jnp.clip(v, -SWIGLU_LIMIT, SWIGLU_LIMIT)
        a = g * jax.nn.sigmoid(SWIGLU_ALPHA * g) * (v + 1.0)
        y = jnp.dot(a.astype(BF16), wd16[...], preferred_element_type=F32) + bd_ref[0, 0]
        o_ref[...] = _pack_pairs(y)


def _expert_call(xs, block_exp, n_used, layer, wg, bg, wu, bu, wd, bd):
    n_slots, dh = xs.shape
    d = 2 * dh
    depth, n_exp, _, f = wg.shape
    te = EXPERT_TILE
    nblk = n_slots // te

    def xmap(b, be, nu):
        return (jnp.minimum(b, nu[0] - 1), 0)

    def wmap(b, be, nu):
        return (layer, be[b], 0, 0)

    return pl.pallas_call(
        _expert_body,
        out_shape=jax.ShapeDtypeStruct((n_slots, dh), U32),
        grid_spec=pltpu.PrefetchScalarGridSpec(
            num_scalar_prefetch=2,
            grid=(nblk,),
            in_specs=[
                pl.BlockSpec((te, dh), xmap),
                pl.BlockSpec((1, 1, d, f), wmap),
                pl.BlockSpec((1, 1, 1, f), wmap),
                pl.BlockSpec((1, 1, d, f), wmap),
                pl.BlockSpec((1, 1, 1, f), wmap),
                pl.BlockSpec((1, 1, f, d), wmap),
                pl.BlockSpec((1, 1, 1, d), wmap),
            ],
            out_specs=pl.BlockSpec((te, dh), xmap),
            scratch_shapes=[pltpu.VMEM((d, f), BF16), pltpu.VMEM((d, f), BF16),
                            pltpu.VMEM((f, d), BF16)],
        ),
        compiler_params=pltpu.CompilerParams(dimension_semantics=("arbitrary",),
                                             vmem_limit_bytes=EXPERT_VMEM_LIMIT),
        name="moe_experts",
    )(block_exp, n_used, xs, wg, bg.reshape(depth, n_exp, 1, f), wu, bu.reshape(depth, n_exp, 1, f),
      wd, bd.reshape(depth, n_exp, 1, d))


def _combine_body(dest_ref, dnext_ref, h_ref, meta_ref, modr_ref, fg_ref, ys_ref, o_ref, buf, sem,
                  *, tc, final):
    i = pl.program_id(0)
    slot = i % 2

    def gather(d_ref, s):
        def issue(r, c):
            for k in range(TOP_K):
                pltpu.make_async_copy(ys_ref.at[pl.ds(d_ref[TOP_K * r + k], 1)],
                                      buf.at[s, k, pl.ds(r, 1)], sem.at[s]).start()
            return c

        lax.fori_loop(0, tc, issue, 0, unroll=DMA_UNROLL)

    @pl.when(i == 0)
    def _():
        gather(dest_ref, 0)

    @pl.when(i + 1 < pl.num_programs(0))
    def _():
        gather(dnext_ref, 1 - slot)

    _rows_wait(ys_ref, TOP_K * tc, sem.at[slot])

    meta = meta_ref[...]
    acc_lo = acc_hi = None
    for k in range(TOP_K):
        pk = meta[:, 2 * TOP_K + k:2 * TOP_K + k + 1]
        y_lo, y_hi = _unpack_pairs(buf[slot, k])
        acc_lo = pk * y_lo if k == 0 else acc_lo + pk * y_lo
        acc_hi = pk * y_hi if k == 0 else acc_hi + pk * y_hi
    out = h_ref[...] + modr_ref[0][5:6] * jnp.concatenate([acc_lo, acc_hi], axis=1)
    if final:
        ms = jnp.mean(out * out, axis=-1, keepdims=True)
        out = out * lax.rsqrt(ms + NORM_EPS) * fg_ref[...]
    o_ref[...] = out


def _combine_call(h, meta, modr, final_g, ys, dest, seq, final):
    n, d = h.shape
    tc = min(ROW_TILE // 2, seq)
    tpb = seq // tc
    steps = n // tc
    return pl.pallas_call(
        functools.partial(_combine_body, tc=tc, final=final),
        out_shape=jax.ShapeDtypeStruct((n, d), F32),
        grid=(steps,),
        in_specs=[
            pl.BlockSpec((TOP_K * tc,), lambda i: (i,), memory_space=pltpu.SMEM),
            pl.BlockSpec((TOP_K * tc,), lambda i: (jnp.minimum(i + 1, steps - 1),),
                         memory_space=pltpu.SMEM),
            pl.BlockSpec((tc, d), lambda i: (i, 0)),
            pl.BlockSpec((tc, LANES), lambda i: (i, 0)),
            pl.BlockSpec((1, SUBLANES, d), lambda i: (i // tpb, 0, 0)),
            _resident((1, d)),
            pl.BlockSpec(memory_space=pl.ANY),
        ],
        out_specs=pl.BlockSpec((tc, d), lambda i: (i, 0)),
        scratch_shapes=[pltpu.VMEM((2, TOP_K, tc, d // 2), U32), pltpu.SemaphoreType.DMA((2,))],
        compiler_params=pltpu.CompilerParams(dimension_semantics=("arbitrary",),
                                             vmem_limit_bytes=VMEM_LIMIT,
                                             disable_bounds_checks=True),
        name="moe_combine",
    )(dest, dest, h, meta, modr, final_g, ys)


def _moe(h, modr, ng, w_router, b_router, layer, wg, bg, wu, bu, wd, bd, final_g, seq, final):
    n, d = h.shape
    n_exp = w_router.shape[1]
    te = EXPERT_TILE
    n_slots = n * TOP_K + n_exp * te
    nblk = n_slots // te
    u, meta, cnt = _router_call(h, modr, ng, w_router, b_router, seq)

    idx = meta[:, 0:TOP_K].astype(jnp.int32)
    rank = meta[:, TOP_K:2 * TOP_K].astype(jnp.int32)
    counts = cnt[0, :n_exp].astype(jnp.int32)
    padded = (counts + te - 1) // te * te
    pad_ends = jnp.cumsum(padded)
    pad_starts = pad_ends - padded
    dest = (pad_starts[idx] + rank).reshape(-1)
    n_used = pad_ends[-1] // te
    blk = jnp.minimum(jnp.arange(nblk, dtype=jnp.int32), n_used - 1)
    block_exp = jnp.minimum(jnp.sum(pad_ends[None, :] <= (blk * te)[:, None], axis=1),
                            n_exp - 1).astype(jnp.int32)
    within = jnp.arange(te, dtype=jnp.int32)[None, :]
    n_fill = (padded - counts)[:, None]
    fill_rows = jnp.where(within < n_fill, (pad_starts + counts)[:, None] + within, n_slots)
    pad_rows = jnp.sort(fill_rows.reshape(-1)).astype(jnp.int32)
    n_pad = jnp.sum(padded - counts).astype(jnp.int32).reshape(1)

    xs = _dispatch_call(u, dest, pad_rows, n_pad, n_slots)
    ys = _expert_call(xs, block_exp, n_used.astype(jnp.int32).reshape(1), layer,
                      wg, bg, wu, bu, wd, bd)
    return _combine_call(h, meta, modr, final_g, ys, dest, seq, final)


def kernel(x, c, ada_w, ada_b, norm_mix_g, norm_ffn_g, cv_w_in, cv_sc_w, cv_cf_w, cv_cf_b, cv_ln_g, cv_ln_b, cv_w_out, s5_lam_re, s5_lam_im, s5_log_dt, s5_b_re, s5_b_im, s5_c_re, s5_c_im, s5_d, s5_w_glu, router_w, router_b, exp_w_gate, exp_b_gate, exp_w_up, exp_b_up, exp_w_down, exp_b_down, final_g):
    bsz, seq, d = x.shape
    depth = ada_w.shape[0]
    n = bsz * seq
    assert bsz <= SUBLANES
    h = x.reshape(n, d)
    mod = _mod_call(c, ada_w, ada_b)
    fg = final_g.reshape(1, d)
    for layer in range(depth):
        modr = jnp.zeros((bsz, SUBLANES, d), F32).at[:, :6].set(mod[layer, :bsz].reshape(bsz, 6, d))
        ng = norm_mix_g[layer].reshape(1, d)
        j = layer // 2
        if layer % 2 == 0:
            h = _conv_mixer_call(h, modr, ng, cv_w_in[j], cv_sc_w[j], cv_cf_w[j], cv_cf_b[j],
                                 cv_ln_g[j], cv_ln_b[j], cv_w_out[j], bsz, seq)
        else:
            h = _s5_mixer(h, modr, ng, s5_lam_re[j], s5_lam_im[j], s5_log_dt[j], s5_b_re[j],
                          s5_b_im[j], s5_c_re[j], s5_c_im[j], s5_d[j], s5_w_glu[j], bsz, seq)
        h = _moe(h, modr, norm_ffn_g[layer].reshape(1, d), router_w[layer], router_b[layer],
                 layer, exp_w_gate, exp_b_gate, exp_w_up, exp_b_up, exp_w_down, exp_b_down, fg, seq,
                 layer == depth - 1)
    return h.reshape(bsz, seq, d)
```

```python
import functools

import jax
import jax.numpy as jnp
from jax import lax
from jax.experimental import pallas as pl
from jax.experimental.pallas import tpu as pltpu

F32 = jnp.float32
BF16 = jnp.bfloat16
NORM_EPS = 1e-5
TOP_K = 4
SWIGLU_LIMIT = 7.0
SWIGLU_ALPHA = 1.702
LANES = 128
SUBLANES = 8
VMEM_LIMIT = 56 * 1024 * 1024
EXPERT_VMEM_LIMIT = 60 * 1024 * 1024
S5_CHUNK = 16
ROW_TILE = 256
EXPERT_TILE = 256
DMA_UNROLL = 8
NEG_BIG = -3.0e38


def _cparams(sem):
    return pltpu.CompilerParams(dimension_semantics=sem, vmem_limit_bytes=VMEM_LIMIT)


def _largest_divisor(n, candidates):
    for c in candidates:
        if n % c == 0:
            return c
    return n


def _norm_mod(x, g, shift, scale):
    ms = jnp.mean(x * x, axis=-1, keepdims=True)
    return x * lax.rsqrt(ms + NORM_EPS) * g * (1.0 + scale) + shift


def _resident(shape):
    nd = len(shape)
    return pl.BlockSpec(shape, lambda *_: (0,) * nd, pipeline_mode=pl.Buffered(1))


def _mod_body(c_ref, w_ref, b_ref, o_ref):
    c = c_ref[...]
    ca = (c * jax.nn.sigmoid(c)).astype(BF16)
    o_ref[0] = jnp.dot(ca, w_ref[0].astype(BF16), preferred_element_type=F32) + b_ref[0]


def _mod_call(c, ada_w, ada_b):
    depth, d, w = ada_w.shape
    b = c.shape[0]
    c8 = jnp.zeros((SUBLANES, d), F32).at[:b].set(c)
    tn = _largest_divisor(w, (1536, 1024, 768, 512, 384, 256, 128))
    return pl.pallas_call(
        _mod_body,
        out_shape=jax.ShapeDtypeStruct((depth, SUBLANES, w), F32),
        grid=(depth, w // tn),
        in_specs=[
            pl.BlockSpec((SUBLANES, d), lambda l, j: (0, 0)),
            pl.BlockSpec((1, d, tn), lambda l, j: (l, 0, j)),
            pl.BlockSpec((1, 1, tn), lambda l, j: (l, 0, j)),
        ],
        out_specs=pl.BlockSpec((1, SUBLANES, tn), lambda l, j: (l, 0, j)),
        compiler_params=_cparams(("arbitrary", "arbitrary")),
        name="adaln_mod",
    )(c8, ada_w, ada_b.reshape(depth, 1, w))


def _conv_body(h_ref, modr_ref, ng_ref, win_ref, scw_ref, cfw_ref, cfv_ref, wout_ref, o_ref,
               gbuf, cvbuf, ycat, wbc, *, tm, sc, cf, ka, kb, ha, hb, rc):
    @pl.when((pl.program_id(0) == 0) & (pl.program_id(1) == 0))
    def _():
        for k in range(kb):
            wbc[k] = jnp.broadcast_to(cfw_ref[k:k + 1, :], (SUBLANES, cf))

    @pl.when(pl.program_id(1) == 0)
    def _():
        gbuf[0, 0:hb, :] = jnp.zeros((hb, cf), F32)
        cvbuf[0:ha, :] = jnp.zeros((ha, sc), F32)

    x = h_ref[...]
    m = modr_ref[0]
    u16 = _norm_mod(x, ng_ref[...], m[0:1], m[1:2]).astype(BF16)

    zb = jnp.dot(u16, win_ref[:, 3 * sc:3 * sc + 2 * cf], preferred_element_type=F32)
    gbuf[0, hb:hb + tm, :] = zb[:, 0:cf] * jax.nn.sigmoid(zb[:, cf:2 * cf])
    span = tm + hb - SUBLANES
    for b in range(1, SUBLANES):
        for c0 in range(0, span, 64):
            c1 = min(c0 + 64, span)
            gbuf[b, c0:c1, :] = gbuf[0, b + c0:b + c1, :]

    za = jnp.dot(u16, win_ref[:, 0:3 * sc], preferred_element_type=F32)
    cv = za[:, sc:2 * sc] * za[:, 2 * sc:3 * sc]
    cvbuf[ha:ha + tm, :] = cv
    acc = scw_ref[ka - 1:ka, :] * cv
    for k in range(ka - 1):
        off = ha - (ka - 1) + k
        acc = acc + scw_ref[k:k + 1, :] * cvbuf[off:off + tm, :]
    ycat[:, 0:sc] = (za[:, 0:sc] * acc).astype(BF16)
    y = jnp.dot(ycat[:, 0:sc], wout_ref[0:sc, :], preferred_element_type=F32)

    cf_b = cfv_ref[0:1, :]
    ln_g = cfv_ref[1:2, :]
    ln_b = cfv_ref[2:3, :]
    for r0 in range(0, tm, rc):
        rows = range(r0, r0 + rc, SUBLANES)
        w_last = wbc[kb - 1]
        accs = [w_last * gbuf[0, hb + r:hb + r + SUBLANES, :] for r in rows]
        for k in range(kb - 1):
            off = hb - (kb - 1) + k
            al = off // SUBLANES * SUBLANES
            wk = wbc[k]
            for c, r in enumerate(rows):
                accs[c] = accs[c] + wk * gbuf[off % SUBLANES, al + r:al + r + SUBLANES, :]
        a = jnp.concatenate(accs, axis=0) + cf_b
        mu = jnp.mean(a, axis=-1, keepdims=True)
        ac = a - mu
        var = jnp.mean(ac * ac, axis=-1, keepdims=True)
        yn = ac * lax.rsqrt(var + NORM_EPS) * ln_g + ln_b
        ycat[r0:r0 + rc, sc:sc + cf] = (yn * jax.nn.sigmoid(yn)).astype(BF16)

    cvbuf[0:ha, :] = cvbuf[tm:tm + ha, :]
    gbuf[0, 0:hb, :] = gbuf[0, tm:tm + hb, :]

    y = y + jnp.dot(ycat[:, sc:sc + cf], wout_ref[sc:sc + cf, :], preferred_element_type=F32)
    o_ref[...] = x + m[2:3] * y


def _conv_mixer_call(h, modr, ng, w_in, sc_w, cf_w, cf_b, ln_g, ln_b, w_out, bsz, seq):
    n, d = h.shape
    ka, sc = sc_w.shape
    kb, cf = cf_w.shape
    ha = -(-(ka - 1) // SUBLANES) * SUBLANES
    hb = -(-(kb - 1) // SUBLANES) * SUBLANES
    tm = min(ROW_TILE, seq)
    assert seq % tm == 0 and tm >= hb and tm >= ha
    ns = seq // tm
    scw = jnp.zeros((SUBLANES, sc), F32).at[:ka].set(sc_w)
    cfw = jnp.zeros((-(-kb // SUBLANES) * SUBLANES, cf), F32).at[:kb].set(cf_w)
    cfv = jnp.zeros((SUBLANES, cf), F32).at[0].set(cf_b).at[1].set(ln_g).at[2].set(ln_b)
    body = functools.partial(_conv_body, tm=tm, sc=sc, cf=cf, ka=ka, kb=kb, ha=ha, hb=hb,
                             rc=min(16, tm))
    return pl.pallas_call(
        body,
        out_shape=jax.ShapeDtypeStruct((n, d), F32),
        grid=(bsz, ns),
        in_specs=[
            pl.BlockSpec((tm, d), lambda b, s: (b * ns + s, 0)),
            pl.BlockSpec((1, SUBLANES, d), lambda b, s: (b, 0, 0)),
            _resident((1, d)),
            _resident(w_in.shape),
            _resident(scw.shape),
            _resident(cfw.shape),
            _resident(cfv.shape),
            _resident(w_out.shape),
        ],
        out_specs=pl.BlockSpec((tm, d), lambda b, s: (b * ns + s, 0)),
        scratch_shapes=[
            pltpu.VMEM((SUBLANES, tm + hb, cf), F32),
            pltpu.VMEM((tm + ha, sc), F32),
            pltpu.VMEM((tm, sc + cf), BF16),
            pltpu.VMEM((cfw.shape[0], SUBLANES, cf), F32),
        ],
        compiler_params=_cparams(("arbitrary", "arbitrary")),
        name="conv_mixer",
    )(h, modr, ng, w_in.astype(BF16), scw, cfw, cfv, w_out.astype(BF16))


def _norm_body(h_ref, modr_ref, ng_ref, o_ref):
    m = modr_ref[0]
    o_ref[...] = _norm_mod(h_ref[...], ng_ref[...], m[0:1], m[1:2]).astype(BF16)


def _norm_call(h, modr, ng, seq):
    n, d = h.shape
    tm = min(2 * ROW_TILE, seq)
    tpb = seq // tm
    return pl.pallas_call(
        _norm_body,
        out_shape=jax.ShapeDtypeStruct((n, d), BF16),
        grid=(n // tm,),
        in_specs=[
            pl.BlockSpec((tm, d), lambda i: (i, 0)),
            pl.BlockSpec((1, SUBLANES, d), lambda i: (i // tpb, 0, 0)),
            _resident((1, d)),
        ],
        out_specs=pl.BlockSpec((tm, d), lambda i: (i, 0)),
        compiler_params=_cparams(("arbitrary",)),
        name="s5_norm",
    )(h, modr, ng)


def _s5_prep(lam_re, lam_im, log_dt, b_re, b_im, c_re, c_im, chunk):
    hi = lax.Precision.HIGHEST
    dt = jnp.exp(log_dt)[:, None]
    mag = jnp.exp(lam_re * dt)
    lb_re = mag * jnp.cos(lam_im * dt)
    lb_im = mag * jnp.sin(lam_im * dt)
    den = lam_re * lam_re + lam_im * lam_im
    f_re = ((lb_re - 1) * lam_re + lb_im * lam_im) / den
    f_im = (lb_im * lam_re - (lb_re - 1) * lam_im) / den
    bb_re = f_re[..., None] * b_re - f_im[..., None] * b_im
    bb_im = f_re[..., None] * b_im + f_im[..., None] * b_re
    g, p, hh = bb_re.shape
    prs, pis = [jnp.ones_like(lb_re)], [jnp.zeros_like(lb_im)]
    for _ in range(chunk):
        pr, pi = prs[-1], pis[-1]
        prs.append(pr * lb_re - pi * lb_im)
        pis.append(pr * lb_im + pi * lb_re)
    pr = jnp.stack(prs)
    pi = jnp.stack(pis)
    kmat = (jnp.einsum('ghp,tgp,gpk->tghk', c_re, pr[:chunk], bb_re, precision=hi)
            - jnp.einsum('ghp,tgp,gpk->tghk', c_re, pi[:chunk], bb_im, precision=hi)
            - jnp.einsum('ghp,tgp,gpk->tghk', c_im, pr[:chunk], bb_im, precision=hi)
            - jnp.einsum('ghp,tgp,gpk->tghk', c_im, pi[:chunk], bb_re, precision=hi))
    prr = pr[chunk - 1::-1][:chunk]
    pir = pi[chunk - 1::-1][:chunk]
    w_re = prr[..., None] * bb_re[None] - pir[..., None] * bb_im[None]
    w_im = prr[..., None] * bb_im[None] + pir[..., None] * bb_re[None]
    w_in = jnp.concatenate([w_re.transpose(1, 0, 3, 2), w_im.transpose(1, 0, 3, 2)], axis=-1)
    w_in = w_in.reshape(g, chunk * hh, 2 * p)
    p1r = pr[1:chunk + 1][:, :, None, :]
    p1i = pi[1:chunk + 1][:, :, None, :]
    co_r = c_re[None] * p1r - c_im[None] * p1i
    co_i = -c_re[None] * p1i - c_im[None] * p1r
    w_out = jnp.concatenate([co_r.transpose(1, 3, 0, 2), co_i.transpose(1, 3, 0, 2)], axis=1)
    w_out = w_out.reshape(g, 2 * p, chunk * hh)
    return kmat, w_in, w_out, pr[chunk], pi[chunk]


def _s5_compact_blocks(kmat, w_in, w_out, chunk, hh, p):
    g = kmat.shape[1]
    gpb = LANES // hh
    nlb = g // gpb
    eye = jnp.eye(gpb, dtype=F32)
    k6 = kmat.reshape(chunk, nlb, gpb, hh, hh).transpose(1, 0, 2, 4, 3)
    bd = (k6[:, :, :, :, None, :] * eye[None, None, :, None, :, None]).reshape(
        nlb, chunk, LANES, LANES)
    wi = w_in.reshape(nlb, gpb, chunk, hh, 2, p).transpose(0, 2, 4, 1, 3, 5)
    wi = wi.reshape(nlb, chunk, 2, LANES, p)
    wi_c = jnp.concatenate([wi] * (LANES // p), axis=-1)
    wo_c = w_out.reshape(nlb, gpb, 2, p, chunk * hh).transpose(0, 2, 1, 3, 4)
    t_i = jnp.arange(chunk)
    h_i = jnp.arange(hh)
    rep = ((t_i[:, None, None, None, None] == t_i[None, None, :, None, None])
           & (h_i[None, :, None, None, None] == h_i[None, None, None, None, :]))
    rep = jnp.broadcast_to(rep, (chunk, hh, chunk, gpb, hh)).reshape(chunk * hh, chunk * LANES)
    return bd.astype(BF16), wi_c.astype(BF16), wo_c.astype(BF16), rep.astype(BF16)


def _fold_chunks(u_ref, ubuf, ucat, chunk):
    ubuf[...] = u_ref[...].astype(F32)
    for s in range(chunk):
        ucat[:, s * LANES:(s + 1) * LANES] = ubuf[:, s, :].astype(BF16)


def _s5_state_body(u_ref, wic_ref, ore_ref, oim_ref, uc_ref, ubuf, ucat, wvm, *, chunk, half, hh, p):
    @pl.when(pl.program_id(1) == 0)
    def _():
        row_g = lax.broadcasted_iota(jnp.int32, (LANES, LANES), 0) // hh
        lane_g = lax.broadcasted_iota(jnp.int32, (LANES, LANES), 1) // p
        for c in range(half // LANES):
            mask = row_g == (LANES // p) * c + lane_g
            for s in range(chunk):
                for q in range(2):
                    lo = q * half + c * LANES
                    wvm[s * LANES:(s + 1) * LANES, lo:lo + LANES] = jnp.where(
                        mask, wic_ref[0, s, q], jnp.zeros((LANES, LANES), BF16))

    _fold_chunks(u_ref, ubuf, ucat, chunk)
    uc_ref[0] = ucat[...]
    s = jnp.dot(ucat[...], wvm[...], preferred_element_type=F32)
    ore_ref[...] = s[:, 0:half]
    oim_ref[...] = s[:, half:2 * half]


def _s5_state_call(u3, wi_c, hh, p):
    nc, chunk, d = u3.shape
    nlb = d // LANES
    half = LANES // hh * p
    gp = nlb * half
    wide = chunk * LANES
    cb = _largest_divisor(nc, (256, 128, 64, 32, 16, 8))
    return pl.pallas_call(
        functools.partial(_s5_state_body, chunk=chunk, half=half, hh=hh, p=p),
        out_shape=(jax.ShapeDtypeStruct((nc, gp), F32), jax.ShapeDtypeStruct((nc, gp), F32),
                   jax.ShapeDtypeStruct((nlb, nc, wide), BF16)),
        grid=(nlb, nc // cb),
        in_specs=[
            pl.BlockSpec((cb, chunk, LANES), lambda j, i: (i, 0, j)),
            pl.BlockSpec((1, chunk, 2, LANES, LANES), lambda j, i: (j, 0, 0, 0, 0)),
        ],
        out_specs=(pl.BlockSpec((cb, half), lambda j, i: (i, j)),
                   pl.BlockSpec((cb, half), lambda j, i: (i, j)),
                   pl.BlockSpec((1, cb, wide), lambda j, i: (j, i, 0))),
        scratch_shapes=[pltpu.VMEM((cb, chunk, LANES), F32), pltpu.VMEM((cb, chunk * LANES), BF16),
                        pltpu.VMEM((chunk * LANES, 2 * half), BF16)],
        compiler_params=_cparams(("arbitrary", "arbitrary")),
        name="s5_chunk_state",
    )(u3, wi_c)


def _s5_scan_body(sre_ref, sim_ref, a_ref, ore_ref, oim_ref, carry, *, cb):
    @pl.when(pl.program_id(1) == 0)
    def _():
        carry[...] = jnp.zeros_like(carry)

    ar = a_ref[0]
    ai = a_ref[1]

    def step(c, st):
        xr, xi = st
        ore_ref[c] = xr.astype(ore_ref.dtype)
        oim_ref[c] = xi.astype(oim_ref.dtype)
        return (ar * xr - ai * xi + sre_ref[c], ar * xi + ai * xr + sim_ref[c])

    xr, xi = lax.fori_loop(0, cb, step, (carry[0], carry[1]))
    carry[0] = xr
    carry[1] = xi


def _s5_scan_call(s_re, s_im, a2, bsz):
    nc, r, _ = s_re.shape
    ncb = nc // bsz
    cb = _largest_divisor(ncb, (64, 32, 16, 8, 4, 2, 1))
    nb = ncb // cb
    blk = pl.BlockSpec((cb, r, LANES), lambda b, j: (b * nb + j, 0, 0))
    return pl.pallas_call(
        functools.partial(_s5_scan_body, cb=cb),
        out_shape=(jax.ShapeDtypeStruct(s_re.shape, BF16), jax.ShapeDtypeStruct(s_re.shape, BF16)),
        grid=(bsz, nb),
        in_specs=[blk, blk, _resident(a2.shape)],
        out_specs=(blk, blk),
        scratch_shapes=[pltpu.VMEM((2, r, LANES), F32)],
        compiler_params=_cparams(("arbitrary", "arbitrary")),
        name="s5_state_scan",
    )(s_re, s_im, a2)


def _s5_out_body(uc_ref, bd_ref, xre_ref, xim_ref, woc_ref, rep_ref, o_ref, ybuf, tvm, wovm,
                 *, chunk, half, hh, p):
    @pl.when(pl.program_id(1) == 0)
    def _():
        zero = jnp.zeros((LANES, LANES), BF16)
        for s in range(chunk):
            for t in range(chunk):
                tvm[s * LANES:(s + 1) * LANES, t * LANES:(t + 1) * LANES] = (
                    bd_ref[0, t - s] if t >= s else zero)
        lane_g = lax.broadcasted_iota(jnp.int32, (p, chunk * LANES), 1) % LANES // hh
        gpb = LANES // hh
        for q in range(2):
            for a in range(gpb):
                spread = jnp.dot(woc_ref[0, q, a], rep_ref[...], preferred_element_type=F32)
                wovm[(q * gpb + a) * p:(q * gpb + a + 1) * p, :] = jnp.where(
                    lane_g == a, spread, 0.0).astype(BF16)

    y = jnp.dot(uc_ref[0], tvm[...], preferred_element_type=F32)
    y = y + jnp.dot(xre_ref[...], wovm[0:half, :], preferred_element_type=F32)
    y = y + jnp.dot(xim_ref[...], wovm[half:2 * half, :], preferred_element_type=F32)
    for t in range(chunk):
        ybuf[:, t, :] = y[:, t * LANES:(t + 1) * LANES]
    o_ref[...] = ybuf[...].astype(BF16)


def _s5_out_call(ucat, bd, xs_re, xs_im, wo_c, rep, hh, p):
    nlb, nc, wide = ucat.shape
    chunk = wide // LANES
    d = nlb * LANES
    gpb = LANES // hh
    half = gpb * p
    cb = _largest_divisor(nc, (256, 128, 64, 32, 16, 8))
    return pl.pallas_call(
        functools.partial(_s5_out_body, chunk=chunk, half=half, hh=hh, p=p),
        out_shape=jax.ShapeDtypeStruct((nc, chunk, d), BF16),
        grid=(nlb, nc // cb),
        in_specs=[
            pl.BlockSpec((1, cb, wide), lambda j, i: (j, i, 0)),
            pl.BlockSpec((1, chunk, LANES, LANES), lambda j, i: (j, 0, 0, 0)),
            pl.BlockSpec((cb, half), lambda j, i: (i, j)),
            pl.BlockSpec((cb, half), lambda j, i: (i, j)),
            pl.BlockSpec((1, 2, gpb, p, chunk * hh), lambda j, i: (j, 0, 0, 0, 0)),
            _resident(rep.shape),
        ],
        out_specs=pl.BlockSpec((cb, chunk, LANES), lambda j, i: (i, 0, j)),
        scratch_shapes=[pltpu.VMEM((cb, chunk, LANES), F32), pltpu.VMEM((wide, wide), BF16),
                        pltpu.VMEM((2 * half, wide), BF16)],
        compiler_params=_cparams(("arbitrary", "arbitrary")),
        name="s5_chunk_out",
    )(ucat, bd, xs_re, xs_im, wo_c, rep)


def _s5_post_body(h_ref, y_ref, modr_ref, ng_ref, dsk_ref, wglu_ref, o_ref, *, d):
    x = h_ref[...]
    m = modr_ref[0]
    u = _norm_mod(x, ng_ref[...], m[0:1], m[1:2])
    yy = jax.nn.gelu(y_ref[...].astype(F32) + dsk_ref[...] * u)
    r = jnp.dot(yy.astype(BF16), wglu_ref[...], preferred_element_type=F32)
    o_ref[...] = x + m[2:3] * (r[:, 0:d] * jax.nn.sigmoid(r[:, d:2 * d]))


def _s5_post_call(h, y, modr, ng, d_skip, w_glu, seq):
    n, d = h.shape
    tm = min(ROW_TILE, seq)
    tpb = seq // tm
    return pl.pallas_call(
        functools.partial(_s5_post_body, d=d),
        out_shape=jax.ShapeDtypeStruct((n, d), F32),
        grid=(n // tm,),
        in_specs=[
            pl.BlockSpec((tm, d), lambda i: (i, 0)),
            pl.BlockSpec((tm, d), lambda i: (i, 0)),
            pl.BlockSpec((1, SUBLANES, d), lambda i: (i // tpb, 0, 0)),
            _resident((1, d)),
            _resident((1, d)),
            _resident(w_glu.shape),
        ],
        out_specs=pl.BlockSpec((tm, d), lambda i: (i, 0)),
        compiler_params=_cparams(("arbitrary",)),
        name="s5_glu",
    )(h, y, modr, ng, d_skip, w_glu.astype(BF16))


def _s5_mixer(h, modr, ng, lam_re, lam_im, log_dt, b_re, b_im, c_re, c_im, d_skip, w_glu,
              bsz, seq):
    n, d = h.shape
    g, p, hh = b_re.shape
    chunk = min(S5_CHUNK, seq)
    nc = n // chunk
    kmat, w_in, w_out, a_re, a_im = _s5_prep(lam_re, lam_im, log_dt, b_re, b_im, c_re, c_im, chunk)
    bd, wi_c, wo_c, rep = _s5_compact_blocks(kmat, w_in, w_out, chunk, hh, p)
    u3 = _norm_call(h, modr, ng, seq).reshape(nc, chunk, d)
    s_re, s_im, ucat = _s5_state_call(u3, wi_c, hh, p)
    r = g * p // LANES
    a2 = jnp.stack([a_re, a_im]).reshape(2, r, LANES)
    xs_re, xs_im = _s5_scan_call(s_re.reshape(nc, r, LANES), s_im.reshape(nc, r, LANES), a2, bsz)
    y = _s5_out_call(ucat, bd, xs_re.reshape(nc, g * p), xs_im.reshape(nc, g * p), wo_c, rep,
                     hh, p).reshape(n, d)
    return _s5_post_call(h, y, modr, ng, d_skip.reshape(1, d), w_glu, seq)


U32 = jnp.uint32
HI_MASK = 0xFFFF0000


def _pack_pairs(x):
    half = x.shape[1] // 2
    lo = lax.bitcast_convert_type(x[:, :half].astype(BF16).astype(F32), U32)
    hi = lax.bitcast_convert_type(x[:, half:].astype(BF16).astype(F32), U32)
    return (lo >> 16) | (hi & U32(HI_MASK))


def _unpack_pairs(w):
    lo = lax.bitcast_convert_type(w << 16, F32)
    hi = lax.bitcast_convert_type(w & U32(HI_MASK), F32)
    return lo, hi


def _router_body(h_ref, modr_ref, ng_ref, whi_ref, wlo_ref, rb_ref, u_ref, meta_ref, cnt_ref,
                 carry, *, n_exp, tm):
    @pl.when(pl.program_id(0) == 0)
    def _():
        carry[...] = jnp.zeros_like(carry)

    m = modr_ref[0]
    u = _norm_mod(h_ref[...], ng_ref[...], m[3:4], m[4:5])
    u_ref[...] = _pack_pairs(u)
    uh = u.astype(BF16)
    ul = (u - uh.astype(F32)).astype(BF16)
    whi = whi_ref[...]
    logits = (jnp.dot(uh, whi, preferred_element_type=F32)
              + jnp.dot(ul, whi, preferred_element_type=F32)
              + jnp.dot(uh, wlo_ref[...], preferred_element_type=F32)) + rb_ref[...]
    lane = lax.broadcasted_iota(jnp.int32, (tm, LANES), 1)
    lane_f = lane.astype(F32)
    l = jnp.where(lane < n_exp, logits, NEG_BIG)
    tops, idxs, sels = [], [], []
    for _ in range(TOP_K):
        mk = jnp.max(l, axis=-1, keepdims=True)
        ik = jnp.min(jnp.where(l == mk, lane_f, 1.0e9), axis=-1, keepdims=True)
        sk = lane_f == ik
        l = jnp.where(sk, NEG_BIG, l)
        tops.append(mk)
        idxs.append(ik)
        sels.append(sk)
    es = [jnp.exp(t - tops[0]) for t in tops]
    den = es[0]
    for e in es[1:]:
        den = den + e
    chosen = jnp.zeros((tm, LANES), F32)
    for sk in sels:
        chosen = jnp.where(sk, 1.0, chosen)
    row = lax.broadcasted_iota(jnp.int32, (tm, tm), 0)
    col = lax.broadcasted_iota(jnp.int32, (tm, tm), 1)
    tri = jnp.where(row > col, 1.0, 0.0).astype(BF16)
    rank_mat = jnp.dot(tri, chosen.astype(BF16), preferred_element_type=F32) + carry[0:1, :]
    meta = jnp.zeros((tm, LANES), F32)
    for k in range(TOP_K):
        rk = jnp.sum(jnp.where(sels[k], rank_mat, 0.0), axis=-1, keepdims=True)
        meta = jnp.where(lane == k, idxs[k], meta)
        meta = jnp.where(lane == TOP_K + k, rk, meta)
        meta = jnp.where(lane == 2 * TOP_K + k, es[k] / den, meta)
    meta_ref[...] = meta
    carry[...] = carry[...] + jnp.sum(chosen, axis=0, keepdims=True)
    cnt_ref[...] = carry[...]


def _router_call(h, modr, ng, w_router, b_router, seq):
    n, d = h.shape
    n_exp = w_router.shape[1]
    tm = min(ROW_TILE, seq)
    tpb = seq // tm
    wpad = jnp.zeros((d, LANES), F32).at[:, :n_exp].set(w_router)
    whi = wpad.astype(BF16)
    wlo = (wpad - whi.astype(F32)).astype(BF16)
    rb = jnp.zeros((1, LANES), F32).at[0, :n_exp].set(b_router)
    return pl.pallas_call(
        functools.partial(_router_body, n_exp=n_exp, tm=tm),
        out_shape=(
            jax.ShapeDtypeStruct((n, d // 2), U32),
            jax.ShapeDtypeStruct((n, LANES), F32),
            jax.ShapeDtypeStruct((SUBLANES, LANES), F32),
        ),
        grid=(n // tm,),
        in_specs=[
            pl.BlockSpec((tm, d), lambda i: (i, 0)),
            pl.BlockSpec((1, SUBLANES, d), lambda i: (i // tpb, 0, 0)),
            _resident((1, d)),
            _resident(whi.shape),
            _resident(wlo.shape),
            _resident(rb.shape),
        ],
        out_specs=(
            pl.BlockSpec((tm, d // 2), lambda i: (i, 0)),
            pl.BlockSpec((tm, LANES), lambda i: (i, 0)),
            pl.BlockSpec((SUBLANES, LANES), lambda i: (0, 0)),
        ),
        scratch_shapes=[pltpu.VMEM((SUBLANES, LANES), F32)],
        compiler_params=_cparams(("arbitrary",)),
        name="moe_router",
    )(h, modr, ng, whi, wlo, rb)


def _row_copy(src_ref, src_row, dst_ref, dst_row, sem):
    return pltpu.make_async_copy(src_ref.at[pl.ds(src_row, 1)], dst_ref.at[pl.ds(dst_row, 1)], sem)


def _rows_wait(hbm_ref, rows, sem):
    pltpu.make_async_copy(hbm_ref.at[pl.ds(0, rows)], hbm_ref.at[pl.ds(0, rows)], sem).wait()


def _dispatch_body(npad_ref, padrow_ref, dest_ref, u_ref, xs_ref, zrow, sem, *, td, pp):
    i = pl.program_id(0)
    zrow[...] = jnp.zeros_like(zrow)
    n_pad = jnp.clip(npad_ref[0] - i * pp, 0, pp)

    def issue(r, c):
        for k in range(TOP_K):
            _row_copy(u_ref, r, xs_ref, dest_ref[TOP_K * r + k], sem).start()
        return c

    lax.fori_loop(0, td, issue, 0, unroll=DMA_UNROLL)

    def issue_pad(j, c):
        _row_copy(zrow, 0, xs_ref, padrow_ref[i * pp + j], sem).start()
        return c

    lax.fori_loop(0, n_pad, issue_pad, 0)

    _rows_wait(xs_ref, TOP_K * td, sem)

    def wait_one(j, c):
        _row_copy(zrow, 0, xs_ref, 0, sem).wait()
        return c

    lax.fori_loop(0, n_pad, wait_one, 0)


def _dispatch_call(u, dest, pad_rows, n_pad, n_slots):
    n, d = u.shape
    td = min(ROW_TILE, n)
    steps = n // td
    pp = -(-pad_rows.shape[0] // steps)
    return pl.pallas_call(
        functools.partial(_dispatch_body, td=td, pp=pp),
        out_shape=jax.ShapeDtypeStruct((n_slots, d), u.dtype),
        grid_spec=pltpu.PrefetchScalarGridSpec(
            num_scalar_prefetch=2,
            grid=(steps,),
            in_specs=[
                pl.BlockSpec((TOP_K * td,), lambda i, npad, prow: (i,), memory_space=pltpu.SMEM),
                pl.BlockSpec((td, d), lambda i, npad, prow: (i, 0)),
            ],
            out_specs=pl.BlockSpec(memory_space=pl.ANY),
            scratch_shapes=[pltpu.VMEM((SUBLANES, d), u.dtype), pltpu.SemaphoreType.DMA(())],
        ),
        compiler_params=pltpu.CompilerParams(dimension_semantics=("arbitrary",),
                                             vmem_limit_bytes=VMEM_LIMIT, has_side_effects=True,
                                             disable_bounds_checks=True),
        name="moe_dispatch",
    )(n_pad, pad_rows, dest, u)


def _expert_body(bexp_ref, nused_ref, x_ref, wg_ref, bg_ref, wu_ref, bu_ref, wd_ref, bd_ref, o_ref,
                 wg16, wu16, wd16):
    @pl.when(pl.program_id(0) < nused_ref[0])
    def _():
        b = pl.program_id(0)
        prev = bexp_ref[jnp.maximum(b - 1, 0)]

        @pl.when((b == 0) | (bexp_ref[b] != prev))
        def _():
            wg16[...] = wg_ref[0, 0].astype(BF16)
            wu16[...] = wu_ref[0, 0].astype(BF16)
            wd16[...] = wd_ref[0, 0].astype(BF16)

        x_lo, x_hi = _unpack_pairs(x_ref[...])
        x = jnp.concatenate([x_lo.astype(BF16), x_hi.astype(BF16)], axis=1)
        g = jnp.dot(x, wg16[...], preferred_element_type=F32) + bg_ref[0, 0]
        g = jnp.minimum(g, SWIGLU_LIMIT)
        v = jnp.dot(x, wu16[...], preferred_element_type=F32) + bu_ref[0, 0]
        v = jnp.clip(v, -SWIGLU_LIMIT, SWIGLU_LIMIT)
        a = g * jax.nn.sigmoid(SWIGLU_ALPHA * g) * (v + 1.0)
        y = jnp.dot(a.astype(BF16), wd16[...], preferred_element_type=F32) + bd_ref[0, 0]
        o_ref[...] = _pack_pairs(y)


def _expert_call(xs, block_exp, n_used, layer, wg, bg, wu, bu, wd, bd):
    n_slots, dh = xs.shape
    d = 2 * dh
    depth, n_exp, _, f = wg.shape
    te = EXPERT_TILE
    nblk = n_slots // te

    def xmap(b, be, nu):
        return (jnp.minimum(b, nu[0] - 1), 0)

    def wmap(b, be, nu):
        return (layer, be[b], 0, 0)

    return pl.pallas_call(
        _expert_body,
        out_shape=jax.ShapeDtypeStruct((n_slots, dh), U32),
        grid_spec=pltpu.PrefetchScalarGridSpec(
            num_scalar_prefetch=2,
            grid=(nblk,),
            in_specs=[
                pl.BlockSpec((te, dh), xmap),
                pl.BlockSpec((1, 1, d, f), wmap),
                pl.BlockSpec((1, 1, 1, f), wmap),
                pl.BlockSpec((1, 1, d, f), wmap),
                pl.BlockSpec((1, 1, 1, f), wmap),
                pl.BlockSpec((1, 1, f, d), wmap),
                pl.BlockSpec((1, 1, 1, d), wmap),
            ],
            out_specs=pl.BlockSpec((te, dh), xmap),
            scratch_shapes=[pltpu.VMEM((d, f), BF16), pltpu.VMEM((d, f), BF16),
                            pltpu.VMEM((f, d), BF16)],
        ),
        compiler_params=pltpu.CompilerParams(dimension_semantics=("arbitrary",),
                                             vmem_limit_bytes=EXPERT_VMEM_LIMIT),
        name="moe_experts",
    )(block_exp, n_used, xs, wg, bg.reshape(depth, n_exp, 1, f), wu, bu.reshape(depth, n_exp, 1, f),
      wd, bd.reshape(depth, n_exp, 1, d))


def _combine_body(dest_ref, dnext_ref, h_ref, meta_ref, modr_ref, fg_ref, ys_ref, o_ref, buf, sem,
                  *, tc, final):
    i = pl.program_id(0)
    slot = i % 2

    def gather(d_ref, s):
        def issue(j, carry):
            for c in range(SUBLANES):
                base = TOP_K * (j * SUBLANES + c)
                for k in range(TOP_K):
                    pltpu.make_async_copy(ys_ref.at[pl.ds(d_ref[base + k], 1)],
                                          buf.at[s, k, j, pl.ds(c, 1)], sem.at[s]).start()
            return carry

        lax.fori_loop(0, tc // SUBLANES, issue, 0)

    @pl.when(i == 0)
    def _():
        gather(dest_ref, 0)

    @pl.when(i + 1 < pl.num_programs(0))
    def _():
        gather(dnext_ref, 1 - slot)

    _rows_wait(ys_ref, TOP_K * tc, sem.at[slot])

    meta = meta_ref[...]
    acc_lo = acc_hi = None
    for k in range(TOP_K):
        pk = meta[:, 2 * TOP_K + k:2 * TOP_K + k + 1]
        y_lo, y_hi = _unpack_pairs(buf[slot, k].reshape(tc, -1))
        acc_lo = pk * y_lo if k == 0 else acc_lo + pk * y_lo
        acc_hi = pk * y_hi if k == 0 else acc_hi + pk * y_hi
    out = h_ref[...] + modr_ref[0][5:6] * jnp.concatenate([acc_lo, acc_hi], axis=1)
    if final:
        ms = jnp.mean(out * out, axis=-1, keepdims=True)
        out = out * lax.rsqrt(ms + NORM_EPS) * fg_ref[...]
    o_ref[...] = out


def _combine_call(h, meta, modr, final_g, ys, dest, seq, final):
    n, d = h.shape
    tc = min(ROW_TILE // 2, seq)
    tpb = seq // tc
    steps = n // tc
    return pl.pallas_call(
        functools.partial(_combine_body, tc=tc, final=final),
        out_shape=jax.ShapeDtypeStruct((n, d), F32),
        grid=(steps,),
        in_specs=[
            pl.BlockSpec((TOP_K * tc,), lambda i: (i,), memory_space=pltpu.SMEM),
            pl.BlockSpec((TOP_K * tc,), lambda i: (jnp.minimum(i + 1, steps - 1),),
                         memory_space=pltpu.SMEM),
            pl.BlockSpec((tc, d), lambda i: (i, 0)),
            pl.BlockSpec((tc, LANES), lambda i: (i, 0)),
            pl.BlockSpec((1, SUBLANES, d), lambda i: (i // tpb, 0, 0)),
            _resident((1, d)),
            pl.BlockSpec(memory_space=pl.ANY),
        ],
        out_specs=pl.BlockSpec((tc, d), lambda i: (i, 0)),
        scratch_shapes=[pltpu.VMEM((2, TOP_K, tc // SUBLANES, SUBLANES, d // 2), U32),
                        pltpu.SemaphoreType.DMA((2,))],
        compiler_params=pltpu.CompilerParams(dimension_semantics=("arbitrary",),
                                             vmem_limit_bytes=VMEM_LIMIT,
                                             disable_bounds_checks=True),
        name="moe_combine",
    )(dest, dest, h, meta, modr, final_g, ys)


def _moe(h, modr, ng, w_router, b_router, layer, wg, bg, wu, bu, wd, bd, final_g, seq, final):
    n, d = h.shape
    n_exp = w_router.shape[1]
    te = EXPERT_TILE
    n_slots = n * TOP_K + n_exp * te
    nblk = n_slots // te
    u, meta, cnt = _router_call(h, modr, ng, w_router, b_router, seq)

    idx = meta[:, 0:TOP_K].astype(jnp.int32)
    rank = meta[:, TOP_K:2 * TOP_K].astype(jnp.int32)
    counts = cnt[0, :n_exp].astype(jnp.int32)
    padded = (counts + te - 1) // te * te
    pad_ends = jnp.cumsum(padded)
    pad_starts = pad_ends - padded
    dest = (pad_starts[idx] + rank).reshape(-1)
    n_used = pad_ends[-1] // te
    blk = jnp.minimum(jnp.arange(nblk, dtype=jnp.int32), n_used - 1)
    block_exp = jnp.minimum(jnp.sum(pad_ends[None, :] <= (blk * te)[:, None], axis=1),
                            n_exp - 1).astype(jnp.int32)
    within = jnp.arange(te, dtype=jnp.int32)[None, :]
    n_fill = (padded - counts)[:, None]
    fill_rows = jnp.where(within < n_fill, (pad_starts + counts)[:, None] + within, n_slots)
    pad_rows = jnp.sort(fill_rows.reshape(-1)).astype(jnp.int32)
    n_pad = jnp.sum(padded - counts).astype(jnp.int32).reshape(1)

    xs = _dispatch_call(u, dest, pad_rows, n_pad, n_slots)
    ys = _expert_call(xs, block_exp, n_used.astype(jnp.int32).reshape(1), layer,
                      wg, bg, wu, bu, wd, bd)
    return _combine_call(h, meta, modr, final_g, ys, dest, seq, final)


def kernel(x, c, ada_w, ada_b, norm_mix_g, norm_ffn_g, cv_w_in, cv_sc_w, cv_cf_w, cv_cf_b, cv_ln_g, cv_ln_b, cv_w_out, s5_lam_re, s5_lam_im, s5_log_dt, s5_b_re, s5_b_im, s5_c_re, s5_c_im, s5_d, s5_w_glu, router_w, router_b, exp_w_gate, exp_b_gate, exp_w_up, exp_b_up, exp_w_down, exp_b_down, final_g):
    bsz, seq, d = x.shape
    depth = ada_w.shape[0]
    n = bsz * seq
    assert bsz <= SUBLANES
    h = x.reshape(n, d)
    mod = _mod_call(c, ada_w, ada_b)
    fg = final_g.reshape(1, d)
    for layer in range(depth):
        modr = jnp.zeros((bsz, SUBLANES, d), F32).at[:, :6].set(mod[layer, :bsz].reshape(bsz, 6, d))
        ng = norm_mix_g[layer].reshape(1, d)
        j = layer // 2
        if layer % 2 == 0:
            h = _conv_mixer_call(h, modr, ng, cv_w_in[j], cv_sc_w[j], cv_cf_w[j], cv_cf_b[j],
                                 cv_ln_g[j], cv_ln_b[j], cv_w_out[j], bsz, seq)
        else:
            h = _s5_mixer(h, modr, ng, s5_lam_re[j], s5_lam_im[j], s5_log_dt[j], s5_b_re[j],
                          s5_b_im[j], s5_c_re[j], s5_c_im[j], s5_d[j], s5_w_glu[j], bsz, seq)
        h = _moe(h, modr, norm_ffn_g[layer].reshape(1, d), router_w[layer], router_b[layer],
                 layer, exp_w_gate, exp_b_gate, exp_w_up, exp_b_up, exp_w_down, exp_b_down, fg, seq,
                 layer == depth - 1)
    return h.reshape(bsz, seq, d)
```

```python
import functools

import jax
import jax.numpy as jnp
from jax import lax
from jax.experimental import pallas as pl
from jax.experimental.pallas import tpu as pltpu

F32 = jnp.float32
BF16 = jnp.bfloat16
NORM_EPS = 1e-5
TOP_K = 4
SWIGLU_LIMIT = 7.0
SWIGLU_ALPHA = 1.702
LANES = 128
SUBLANES = 8
VMEM_LIMIT = 56 * 1024 * 1024
EXPERT_VMEM_LIMIT = 60 * 1024 * 1024
S5_CHUNK = 16
ROW_TILE = 256
EXPERT_TILE = 256
DMA_UNROLL = 8
NEG_BIG = -3.0e38


def _cparams(sem):
    return pltpu.CompilerParams(dimension_semantics=sem, vmem_limit_bytes=VMEM_LIMIT)


def _largest_divisor(n, candidates):
    for c in candidates:
        if n % c == 0:
            return c
    return n


def _norm_mod(x, g, shift, scale):
    ms = jnp.mean(x * x, axis=-1, keepdims=True)
    return x * lax.rsqrt(ms + NORM_EPS) * g * (1.0 + scale) + shift


def _resident(shape):
    nd = len(shape)
    return pl.BlockSpec(shape, lambda *_: (0,) * nd, pipeline_mode=pl.Buffered(1))


def _mod_body(c_ref, w_ref, b_ref, o_ref):
    c = c_ref[...]
    ca = (c * jax.nn.sigmoid(c)).astype(BF16)
    o_ref[0] = jnp.dot(ca, w_ref[0].astype(BF16), preferred_element_type=F32) + b_ref[0]


def _mod_call(c, ada_w, ada_b):
    depth, d, w = ada_w.shape
    b = c.shape[0]
    c8 = jnp.zeros((SUBLANES, d), F32).at[:b].set(c)
    tn = _largest_divisor(w, (1536, 1024, 768, 512, 384, 256, 128))
    return pl.pallas_call(
        _mod_body,
        out_shape=jax.ShapeDtypeStruct((depth, SUBLANES, w), F32),
        grid=(depth, w // tn),
        in_specs=[
            pl.BlockSpec((SUBLANES, d), lambda l, j: (0, 0)),
            pl.BlockSpec((1, d, tn), lambda l, j: (l, 0, j)),
            pl.BlockSpec((1, 1, tn), lambda l, j: (l, 0, j)),
        ],
        out_specs=pl.BlockSpec((1, SUBLANES, tn), lambda l, j: (l, 0, j)),
        compiler_params=_cparams(("arbitrary", "arbitrary")),
        name="adaln_mod",
    )(c8, ada_w, ada_b.reshape(depth, 1, w))


def _conv_body(h_ref, modr_ref, ng_ref, win_ref, scw_ref, cfw_ref, cfv_ref, wout_ref, o_ref,
               gbuf, cvbuf, ycat, wbc, *, tm, sc, cf, ka, kb, ha, hb, rc):
    @pl.when((pl.program_id(0) == 0) & (pl.program_id(1) == 0))
    def _():
        for k in range(kb):
            wbc[k] = jnp.broadcast_to(cfw_ref[k:k + 1, :], (SUBLANES, cf))

    @pl.when(pl.program_id(1) == 0)
    def _():
        gbuf[0, 0:hb, :] = jnp.zeros((hb, cf), F32)
        cvbuf[0:ha, :] = jnp.zeros((ha, sc), F32)

    x = h_ref[...]
    m = modr_ref[0]
    u16 = _norm_mod(x, ng_ref[...], m[0:1], m[1:2]).astype(BF16)

    zb = jnp.dot(u16, win_ref[:, 3 * sc:3 * sc + 2 * cf], preferred_element_type=F32)
    gbuf[0, hb:hb + tm, :] = zb[:, 0:cf] * jax.nn.sigmoid(zb[:, cf:2 * cf])
    span = tm + hb - SUBLANES
    for b in range(1, SUBLANES):
        for c0 in range(0, span, 64):
            c1 = min(c0 + 64, span)
            gbuf[b, c0:c1, :] = gbuf[0, b + c0:b + c1, :]

    za = jnp.dot(u16, win_ref[:, 0:3 * sc], preferred_element_type=F32)
    cv = za[:, sc:2 * sc] * za[:, 2 * sc:3 * sc]
    cvbuf[ha:ha + tm, :] = cv
    acc = scw_ref[ka - 1:ka, :] * cv
    for k in range(ka - 1):
        off = ha - (ka - 1) + k
        acc = acc + scw_ref[k:k + 1, :] * cvbuf[off:off + tm, :]
    ycat[:, 0:sc] = (za[:, 0:sc] * acc).astype(BF16)
    y = jnp.dot(ycat[:, 0:sc], wout_ref[0:sc, :], preferred_element_type=F32)

    cf_b = cfv_ref[0:1, :]
    ln_g = cfv_ref[1:2, :]
    ln_b = cfv_ref[2:3, :]
    for r0 in range(0, tm, rc):
        rows = range(r0, r0 + rc, SUBLANES)
        w_last = wbc[kb - 1]
        accs = [w_last * gbuf[0, hb + r:hb + r + SUBLANES, :] for r in rows]
        for k in range(kb - 1):
            off = hb - (kb - 1) + k
            al = off // SUBLANES * SUBLANES
            wk = wbc[k]
            for c, r in enumerate(rows):
                accs[c] = accs[c] + wk * gbuf[off % SUBLANES, al + r:al + r + SUBLANES, :]
        a = jnp.concatenate(accs, axis=0) + cf_b
        mu = jnp.mean(a, axis=-1, keepdims=True)
        ac = a - mu
        var = jnp.mean(ac * ac, axis=-1, keepdims=True)
        yn = ac * lax.rsqrt(var + NORM_EPS) * ln_g + ln_b
        ycat[r0:r0 + rc, sc:sc + cf] = (yn * jax.nn.sigmoid(yn)).astype(BF16)

    cvbuf[0:ha, :] = cvbuf[tm:tm + ha, :]
    gbuf[0, 0:hb, :] = gbuf[0, tm:tm + hb, :]

    y = y + jnp.dot(ycat[:, sc:sc + cf], wout_ref[sc:sc + cf, :], preferred_element_type=F32)
    o_ref[...] = x + m[2:3] * y


def _conv_mixer_call(h, modr, ng, w_in, sc_w, cf_w, cf_b, ln_g, ln_b, w_out, bsz, seq):
    n, d = h.shape
    ka, sc = sc_w.shape
    kb, cf = cf_w.shape
    ha = -(-(ka - 1) // SUBLANES) * SUBLANES
    hb = -(-(kb - 1) // SUBLANES) * SUBLANES
    tm = min(ROW_TILE, seq)
    assert seq % tm == 0 and tm >= hb and tm >= ha
    ns = seq // tm
    scw = jnp.zeros((SUBLANES, sc), F32).at[:ka].set(sc_w)
    cfw = jnp.zeros((-(-kb // SUBLANES) * SUBLANES, cf), F32).at[:kb].set(cf_w)
    cfv = jnp.zeros((SUBLANES, cf), F32).at[0].set(cf_b).at[1].set(ln_g).at[2].set(ln_b)
    body = functools.partial(_conv_body, tm=tm, sc=sc, cf=cf, ka=ka, kb=kb, ha=ha, hb=hb,
                             rc=min(16, tm))
    return pl.pallas_call(
        body,
        out_shape=jax.ShapeDtypeStruct((n, d), F32),
        grid=(bsz, ns),
        in_specs=[
            pl.BlockSpec((tm, d), lambda b, s: (b * ns + s, 0)),
            pl.BlockSpec((1, SUBLANES, d), lambda b, s: (b, 0, 0)),
            _resident((1, d)),
            _resident(w_in.shape),
            _resident(scw.shape),
            _resident(cfw.shape),
            _resident(cfv.shape),
            _resident(w_out.shape),
        ],
        out_specs=pl.BlockSpec((tm, d), lambda b, s: (b * ns + s, 0)),
        scratch_shapes=[
            pltpu.VMEM((SUBLANES, tm + hb, cf), F32),
            pltpu.VMEM((tm + ha, sc), F32),
            pltpu.VMEM((tm, sc + cf), BF16),
            pltpu.VMEM((cfw.shape[0], SUBLANES, cf), F32),
        ],
        compiler_params=_cparams(("arbitrary", "arbitrary")),
        name="conv_mixer",
    )(h, modr, ng, w_in.astype(BF16), scw, cfw, cfv, w_out.astype(BF16))


def _norm_body(h_ref, modr_ref, ng_ref, o_ref):
    m = modr_ref[0]
    o_ref[...] = _norm_mod(h_ref[...], ng_ref[...], m[0:1], m[1:2]).astype(BF16)


def _norm_call(h, modr, ng, seq):
    n, d = h.shape
    tm = min(2 * ROW_TILE, seq)
    tpb = seq // tm
    return pl.pallas_call(
        _norm_body,
        out_shape=jax.ShapeDtypeStruct((n, d), BF16),
        grid=(n // tm,),
        in_specs=[
            pl.BlockSpec((tm, d), lambda i: (i, 0)),
            pl.BlockSpec((1, SUBLANES, d), lambda i: (i // tpb, 0, 0)),
            _resident((1, d)),
        ],
        out_specs=pl.BlockSpec((tm, d), lambda i: (i, 0)),
        compiler_params=_cparams(("arbitrary",)),
        name="s5_norm",
    )(h, modr, ng)


def _s5_prep(lam_re, lam_im, log_dt, b_re, b_im, c_re, c_im, chunk):
    hi = lax.Precision.HIGHEST
    dt = jnp.exp(log_dt)[:, None]
    mag = jnp.exp(lam_re * dt)
    lb_re = mag * jnp.cos(lam_im * dt)
    lb_im = mag * jnp.sin(lam_im * dt)
    den = lam_re * lam_re + lam_im * lam_im
    f_re = ((lb_re - 1) * lam_re + lb_im * lam_im) / den
    f_im = (lb_im * lam_re - (lb_re - 1) * lam_im) / den
    bb_re = f_re[..., None] * b_re - f_im[..., None] * b_im
    bb_im = f_re[..., None] * b_im + f_im[..., None] * b_re
    g, p, hh = bb_re.shape
    prs, pis = [jnp.ones_like(lb_re)], [jnp.zeros_like(lb_im)]
    for _ in range(chunk):
        pr, pi = prs[-1], pis[-1]
        prs.append(pr * lb_re - pi * lb_im)
        pis.append(pr * lb_im + pi * lb_re)
    pr = jnp.stack(prs)
    pi = jnp.stack(pis)
    kmat = (jnp.einsum('ghp,tgp,gpk->tghk', c_re, pr[:chunk], bb_re, precision=hi)
            - jnp.einsum('ghp,tgp,gpk->tghk', c_re, pi[:chunk], bb_im, precision=hi)
            - jnp.einsum('ghp,tgp,gpk->tghk', c_im, pr[:chunk], bb_im, precision=hi)
            - jnp.einsum('ghp,tgp,gpk->tghk', c_im, pi[:chunk], bb_re, precision=hi))
    prr = pr[chunk - 1::-1][:chunk]
    pir = pi[chunk - 1::-1][:chunk]
    w_re = prr[..., None] * bb_re[None] - pir[..., None] * bb_im[None]
    w_im = prr[..., None] * bb_im[None] + pir[..., None] * bb_re[None]
    w_in = jnp.concatenate([w_re.transpose(1, 0, 3, 2), w_im.transpose(1, 0, 3, 2)], axis=-1)
    w_in = w_in.reshape(g, chunk * hh, 2 * p)
    p1r = pr[1:chunk + 1][:, :, None, :]
    p1i = pi[1:chunk + 1][:, :, None, :]
    co_r = c_re[None] * p1r - c_im[None] * p1i
    co_i = -c_re[None] * p1i - c_im[None] * p1r
    w_out = jnp.concatenate([co_r.transpose(1, 3, 0, 2), co_i.transpose(1, 3, 0, 2)], axis=1)
    w_out = w_out.reshape(g, 2 * p, chunk * hh)
    return kmat, w_in, w_out, pr[chunk], pi[chunk]


def _s5_compact_blocks(kmat, w_in, w_out, chunk, hh, p):
    g = kmat.shape[1]
    gpb = LANES // hh
    nlb = g // gpb
    eye = jnp.eye(gpb, dtype=F32)
    k6 = kmat.reshape(chunk, nlb, gpb, hh, hh).transpose(1, 0, 2, 4, 3)
    bd = (k6[:, :, :, :, None, :] * eye[None, None, :, None, :, None]).reshape(
        nlb, chunk, LANES, LANES)
    wi = w_in.reshape(nlb, gpb, chunk, hh, 2, p).transpose(0, 2, 4, 1, 3, 5)
    wi = wi.reshape(nlb, chunk, 2, LANES, p)
    wi_c = jnp.concatenate([wi] * (LANES // p), axis=-1)
    wo_c = w_out.reshape(nlb, gpb, 2, p, chunk * hh).transpose(0, 2, 1, 3, 4)
    t_i = jnp.arange(chunk)
    h_i = jnp.arange(hh)
    rep = ((t_i[:, None, None, None, None] == t_i[None, None, :, None, None])
           & (h_i[None, :, None, None, None] == h_i[None, None, None, None, :]))
    rep = jnp.broadcast_to(rep, (chunk, hh, chunk, gpb, hh)).reshape(chunk * hh, chunk * LANES)
    return bd.astype(BF16), wi_c.astype(BF16), wo_c.astype(BF16), rep.astype(BF16)


def _fold_chunks(u_ref, ubuf, ucat, chunk):
    ubuf[...] = u_ref[...].astype(F32)
    for s in range(chunk):
        ucat[:, s * LANES:(s + 1) * LANES] = ubuf[:, s, :].astype(BF16)


def _s5_state_body(u_ref, wic_ref, ore_ref, oim_ref, uc_ref, ubuf, ucat, wvm, *, chunk, half, hh, p):
    @pl.when(pl.program_id(1) == 0)
    def _():
        row_g = lax.broadcasted_iota(jnp.int32, (LANES, LANES), 0) // hh
        lane_g = lax.broadcasted_iota(jnp.int32, (LANES, LANES), 1) // p
        for c in range(half // LANES):
            mask = row_g == (LANES // p) * c + lane_g
            for s in range(chunk):
                for q in range(2):
                    lo = q * half + c * LANES
                    wvm[s * LANES:(s + 1) * LANES, lo:lo + LANES] = jnp.where(
                        mask, wic_ref[0, s, q], jnp.zeros((LANES, LANES), BF16))

    _fold_chunks(u_ref, ubuf, ucat, chunk)
    uc_ref[0] = ucat[...]
    s = jnp.dot(ucat[...], wvm[...], preferred_element_type=F32)
    ore_ref[...] = s[:, 0:half]
    oim_ref[...] = s[:, half:2 * half]


def _s5_state_call(u3, wi_c, hh, p):
    nc, chunk, d = u3.shape
    nlb = d // LANES
    half = LANES // hh * p
    gp = nlb * half
    wide = chunk * LANES
    cb = _largest_divisor(nc, (256, 128, 64, 32, 16, 8))
    return pl.pallas_call(
        functools.partial(_s5_state_body, chunk=chunk, half=half, hh=hh, p=p),
        out_shape=(jax.ShapeDtypeStruct((nc, gp), F32), jax.ShapeDtypeStruct((nc, gp), F32),
                   jax.ShapeDtypeStruct((nlb, nc, wide), BF16)),
        grid=(nlb, nc // cb),
        in_specs=[
            pl.BlockSpec((cb, chunk, LANES), lambda j, i: (i, 0, j)),
            pl.BlockSpec((1, chunk, 2, LANES, LANES), lambda j, i: (j, 0, 0, 0, 0)),
        ],
        out_specs=(pl.BlockSpec((cb, half), lambda j, i: (i, j)),
                   pl.BlockSpec((cb, half), lambda j, i: (i, j)),
                   pl.BlockSpec((1, cb, wide), lambda j, i: (j, i, 0))),
        scratch_shapes=[pltpu.VMEM((cb, chunk, LANES), F32), pltpu.VMEM((cb, chunk * LANES), BF16),
                        pltpu.VMEM((chunk * LANES, 2 * half), BF16)],
        compiler_params=_cparams(("arbitrary", "arbitrary")),
        name="s5_chunk_state",
    )(u3, wi_c)


def _s5_scan_body(sre_ref, sim_ref, a_ref, ore_ref, oim_ref, carry, *, cb):
    @pl.when(pl.program_id(1) == 0)
    def _():
        carry[...] = jnp.zeros_like(carry)

    ar = a_ref[0]
    ai = a_ref[1]

    def step(c, st):
        xr, xi = st
        ore_ref[c] = xr.astype(ore_ref.dtype)
        oim_ref[c] = xi.astype(oim_ref.dtype)
        return (ar * xr - ai * xi + sre_ref[c], ar * xi + ai * xr + sim_ref[c])

    xr, xi = lax.fori_loop(0, cb, step, (carry[0], carry[1]))
    carry[0] = xr
    carry[1] = xi


def _s5_scan_call(s_re, s_im, a2, bsz):
    nc, r, _ = s_re.shape
    ncb = nc // bsz
    cb = _largest_divisor(ncb, (64, 32, 16, 8, 4, 2, 1))
    nb = ncb // cb
    blk = pl.BlockSpec((cb, r, LANES), lambda b, j: (b * nb + j, 0, 0))
    return pl.pallas_call(
        functools.partial(_s5_scan_body, cb=cb),
        out_shape=(jax.ShapeDtypeStruct(s_re.shape, BF16), jax.ShapeDtypeStruct(s_re.shape, BF16)),
        grid=(bsz, nb),
        in_specs=[blk, blk, _resident(a2.shape)],
        out_specs=(blk, blk),
        scratch_shapes=[pltpu.VMEM((2, r, LANES), F32)],
        compiler_params=_cparams(("arbitrary", "arbitrary")),
        name="s5_state_scan",
    )(s_re, s_im, a2)


def _s5_out_body(uc_ref, bd_ref, xre_ref, xim_ref, woc_ref, rep_ref, o_ref, ybuf, tvm, wovm,
                 *, chunk, half, hh, p):
    @pl.when(pl.program_id(1) == 0)
    def _():
        zero = jnp.zeros((LANES, LANES), BF16)
        for s in range(chunk):
            for t in range(chunk):
                tvm[s * LANES:(s + 1) * LANES, t * LANES:(t + 1) * LANES] = (
                    bd_ref[0, t - s] if t >= s else zero)
        lane_g = lax.broadcasted_iota(jnp.int32, (p, chunk * LANES), 1) % LANES // hh
        gpb = LANES // hh
        for q in range(2):
            for a in range(gpb):
                spread = jnp.dot(woc_ref[0, q, a], rep_ref[...], preferred_element_type=F32)
                wovm[(q * gpb + a) * p:(q * gpb + a + 1) * p, :] = jnp.where(
                    lane_g == a, spread, 0.0).astype(BF16)

    y = jnp.dot(uc_ref[0], tvm[...], preferred_element_type=F32)
    y = y + jnp.dot(xre_ref[...], wovm[0:half, :], preferred_element_type=F32)
    y = y + jnp.dot(xim_ref[...], wovm[half:2 * half, :], preferred_element_type=F32)
    for t in range(chunk):
        ybuf[:, t, :] = y[:, t * LANES:(t + 1) * LANES]
    o_ref[...] = ybuf[...].astype(BF16)


def _s5_out_call(ucat, bd, xs_re, xs_im, wo_c, rep, hh, p):
    nlb, nc, wide = ucat.shape
    chunk = wide // LANES
    d = nlb * LANES
    gpb = LANES // hh
    half = gpb * p
    cb = _largest_divisor(nc, (256, 128, 64, 32, 16, 8))
    return pl.pallas_call(
        functools.partial(_s5_out_body, chunk=chunk, half=half, hh=hh, p=p),
        out_shape=jax.ShapeDtypeStruct((nc, chunk, d), BF16),
        grid=(nlb, nc // cb),
        in_specs=[
            pl.BlockSpec((1, cb, wide), lambda j, i: (j, i, 0)),
            pl.BlockSpec((1, chunk, LANES, LANES), lambda j, i: (j, 0, 0, 0)),
            pl.BlockSpec((cb, half), lambda j, i: (i, j)),
            pl.BlockSpec((cb, half), lambda j, i: (i, j)),
            pl.BlockSpec((1, 2, gpb, p, chunk * hh), lambda j, i: (j, 0, 0, 0, 0)),
            _resident(rep.shape),
        ],
        out_specs=pl.BlockSpec((cb, chunk, LANES), lambda j, i: (i, 0, j)),
        scratch_shapes=[pltpu.VMEM((cb, chunk, LANES), F32), pltpu.VMEM((wide, wide), BF16),
                        pltpu.VMEM((2 * half, wide), BF16)],
        compiler_params=_cparams(("arbitrary", "arbitrary")),
        name="s5_chunk_out",
    )(ucat, bd, xs_re, xs_im, wo_c, rep)


def _s5_post_body(h_ref, y_ref, modr_ref, ng_ref, dsk_ref, wglu_ref, o_ref, *, d):
    x = h_ref[...]
    m = modr_ref[0]
    u = _norm_mod(x, ng_ref[...], m[0:1], m[1:2])
    yy = jax.nn.gelu(y_ref[...].astype(F32) + dsk_ref[...] * u)
    r = jnp.dot(yy.astype(BF16), wglu_ref[...], preferred_element_type=F32)
    o_ref[...] = x + m[2:3] * (r[:, 0:d] * jax.nn.sigmoid(r[:, d:2 * d]))


def _s5_post_call(h, y, modr, ng, d_skip, w_glu, seq):
    n, d = h.shape
    tm = min(ROW_TILE, seq)
    tpb = seq // tm
    return pl.pallas_call(
        functools.partial(_s5_post_body, d=d),
        out_shape=jax.ShapeDtypeStruct((n, d), F32),
        grid=(n // tm,),
        in_specs=[
            pl.BlockSpec((tm, d), lambda i: (i, 0)),
            pl.BlockSpec((tm, d), lambda i: (i, 0)),
            pl.BlockSpec((1, SUBLANES, d), lambda i: (i // tpb, 0, 0)),
            _resident((1, d)),
            _resident((1, d)),
            _resident(w_glu.shape),
        ],
        out_specs=pl.BlockSpec((tm, d), lambda i: (i, 0)),
        compiler_params=_cparams(("arbitrary",)),
        name="s5_glu",
    )(h, y, modr, ng, d_skip, w_glu.astype(BF16))


def _s5_mixer(h, modr, ng, lam_re, lam_im, log_dt, b_re, b_im, c_re, c_im, d_skip, w_glu,
              bsz, seq):
    n, d = h.shape
    g, p, hh = b_re.shape
    chunk = min(S5_CHUNK, seq)
    nc = n // chunk
    kmat, w_in, w_out, a_re, a_im = _s5_prep(lam_re, lam_im, log_dt, b_re, b_im, c_re, c_im, chunk)
    bd, wi_c, wo_c, rep = _s5_compact_blocks(kmat, w_in, w_out, chunk, hh, p)
    u3 = _norm_call(h, modr, ng, seq).reshape(nc, chunk, d)
    s_re, s_im, ucat = _s5_state_call(u3, wi_c, hh, p)
    r = g * p // LANES
    a2 = jnp.stack([a_re, a_im]).reshape(2, r, LANES)
    xs_re, xs_im = _s5_scan_call(s_re.reshape(nc, r, LANES), s_im.reshape(nc, r, LANES), a2, bsz)
    y = _s5_out_call(ucat, bd, xs_re.reshape(nc, g * p), xs_im.reshape(nc, g * p), wo_c, rep,
                     hh, p).reshape(n, d)
    return _s5_post_call(h, y, modr, ng, d_skip.reshape(1, d), w_glu, seq)


U32 = jnp.uint32
HI_MASK = 0xFFFF0000


def _pack_pairs(x):
    half = x.shape[1] // 2
    lo = lax.bitcast_convert_type(x[:, :half].astype(BF16).astype(F32), U32)
    hi = lax.bitcast_convert_type(x[:, half:].astype(BF16).astype(F32), U32)
    return (lo >> 16) | (hi & U32(HI_MASK))


def _unpack_pairs(w):
    lo = lax.bitcast_convert_type(w << 16, F32)
    hi = lax.bitcast_convert_type(w & U32(HI_MASK), F32)
    return lo, hi


def _router_body(h_ref, modr_ref, ng_ref, whi_ref, wlo_ref, rb_ref, u_ref, meta_ref, cnt_ref,
                 carry, *, n_exp, tm):
    @pl.when(pl.program_id(0) == 0)
    def _():
        carry[...] = jnp.zeros_like(carry)

    m = modr_ref[0]
    u = _norm_mod(h_ref[...], ng_ref[...], m[3:4], m[4:5])
    u_ref[...] = _pack_pairs(u)
    uh = u.astype(BF16)
    ul = (u - uh.astype(F32)).astype(BF16)
    whi = whi_ref[...]
    logits = (jnp.dot(uh, whi, preferred_element_type=F32)
              + jnp.dot(ul, whi, preferred_element_type=F32)
              + jnp.dot(uh, wlo_ref[...], preferred_element_type=F32)) + rb_ref[...]
    lane = lax.broadcasted_iota(jnp.int32, (tm, LANES), 1)
    lane_f = lane.astype(F32)
    l = jnp.where(lane < n_exp, logits, NEG_BIG)
    tops, idxs, sels = [], [], []
    for _ in range(TOP_K):
        mk = jnp.max(l, axis=-1, keepdims=True)
        ik = jnp.min(jnp.where(l == mk, lane_f, 1.0e9), axis=-1, keepdims=True)
        sk = lane_f == ik
        l = jnp.where(sk, NEG_BIG, l)
        tops.append(mk)
        idxs.append(ik)
        sels.append(sk)
    es = [jnp.exp(t - tops[0]) for t in tops]
    den = es[0]
    for e in es[1:]:
        den = den + e
    chosen = jnp.zeros((tm, LANES), F32)
    for sk in sels:
        chosen = jnp.where(sk, 1.0, chosen)
    row = lax.broadcasted_iota(jnp.int32, (tm, tm), 0)
    col = lax.broadcasted_iota(jnp.int32, (tm, tm), 1)
    tri = jnp.where(row > col, 1.0, 0.0).astype(BF16)
    rank_mat = jnp.dot(tri, chosen.astype(BF16), preferred_element_type=F32) + carry[0:1, :]
    meta = jnp.zeros((tm, LANES), F32)
    for k in range(TOP_K):
        rk = jnp.sum(jnp.where(sels[k], rank_mat, 0.0), axis=-1, keepdims=True)
        meta = jnp.where(lane == k, idxs[k], meta)
        meta = jnp.where(lane == TOP_K + k, rk, meta)
        meta = jnp.where(lane == 2 * TOP_K + k, es[k] / den, meta)
    meta_ref[...] = meta
    carry[...] = carry[...] + jnp.sum(chosen, axis=0, keepdims=True)
    cnt_ref[...] = carry[...]


def _router_call(h, modr, ng, w_router, b_router, seq):
    n, d = h.shape
    n_exp = w_router.shape[1]
    tm = min(ROW_TILE, seq)
    tpb = seq // tm
    wpad = jnp.zeros((d, LANES), F32).at[:, :n_exp].set(w_router)
    whi = wpad.astype(BF16)
    wlo = (wpad - whi.astype(F32)).astype(BF16)
    rb = jnp.zeros((1, LANES), F32).at[0, :n_exp].set(b_router)
    return pl.pallas_call(
        functools.partial(_router_body, n_exp=n_exp, tm=tm),
        out_shape=(
            jax.ShapeDtypeStruct((n, d // 2), U32),
            jax.ShapeDtypeStruct((n, LANES), F32),
            jax.ShapeDtypeStruct((SUBLANES, LANES), F32),
        ),
        grid=(n // tm,),
        in_specs=[
            pl.BlockSpec((tm, d), lambda i: (i, 0)),
            pl.BlockSpec((1, SUBLANES, d), lambda i: (i // tpb, 0, 0)),
            _resident((1, d)),
            _resident(whi.shape),
            _resident(wlo.shape),
            _resident(rb.shape),
        ],
        out_specs=(
            pl.BlockSpec((tm, d // 2), lambda i: (i, 0)),
            pl.BlockSpec((tm, LANES), lambda i: (i, 0)),
            pl.BlockSpec((SUBLANES, LANES), lambda i: (0, 0)),
        ),
        scratch_shapes=[pltpu.VMEM((SUBLANES, LANES), F32)],
        compiler_params=_cparams(("arbitrary",)),
        name="moe_router",
    )(h, modr, ng, whi, wlo, rb)


def _row_copy(src_ref, src_row, dst_ref, dst_row, sem):
    return pltpu.make_async_copy(src_ref.at[pl.ds(src_row, 1)], dst_ref.at[pl.ds(dst_row, 1)], sem)


def _rows_wait(hbm_ref, rows, sem):
    pltpu.make_async_copy(hbm_ref.at[pl.ds(0, rows)], hbm_ref.at[pl.ds(0, rows)], sem).wait()


def _dispatch_body(npad_ref, padrow_ref, dest_ref, u_ref, xs_ref, zrow, sem, *, td, pp):
    i = pl.program_id(0)
    zrow[...] = jnp.zeros_like(zrow)
    n_pad = jnp.clip(npad_ref[0] - i * pp, 0, pp)

    def issue(r, c):
        for k in range(TOP_K):
            _row_copy(u_ref, r, xs_ref, dest_ref[TOP_K * r + k], sem).start()
        return c

    lax.fori_loop(0, td, issue, 0, unroll=DMA_UNROLL)

    def issue_pad(j, c):
        _row_copy(zrow, 0, xs_ref, padrow_ref[i * pp + j], sem).start()
        return c

    lax.fori_loop(0, n_pad, issue_pad, 0)

    _rows_wait(xs_ref, TOP_K * td, sem)

    def wait_one(j, c):
        _row_copy(zrow, 0, xs_ref, 0, sem).wait()
        return c

    lax.fori_loop(0, n_pad, wait_one, 0)


def _dispatch_call(u, dest, pad_rows, n_pad, n_slots):
    n, d = u.shape
    td = min(2 * ROW_TILE, n)
    steps = n // td
    pp = -(-pad_rows.shape[0] // steps)
    return pl.pallas_call(
        functools.partial(_dispatch_body, td=td, pp=pp),
        out_shape=jax.ShapeDtypeStruct((n_slots, d), u.dtype),
        grid_spec=pltpu.PrefetchScalarGridSpec(
            num_scalar_prefetch=2,
            grid=(steps,),
            in_specs=[
                pl.BlockSpec((TOP_K * td,), lambda i, npad, prow: (i,), memory_space=pltpu.SMEM),
                pl.BlockSpec((td, d), lambda i, npad, prow: (i, 0)),
            ],
            out_specs=pl.BlockSpec(memory_space=pl.ANY),
            scratch_shapes=[pltpu.VMEM((SUBLANES, d), u.dtype), pltpu.SemaphoreType.DMA(())],
        ),
        compiler_params=pltpu.CompilerParams(dimension_semantics=("arbitrary",),
                                             vmem_limit_bytes=VMEM_LIMIT, has_side_effects=True,
                                             disable_bounds_checks=True),
        name="moe_dispatch",
    )(n_pad, pad_rows, dest, u)


def _expert_body(bexp_ref, nused_ref, x_ref, wg_ref, bg_ref, wu_ref, bu_ref, wd_ref, bd_ref, o_ref,
                 wg16, wu16, wd16):
    @pl.when(pl.program_id(0) < nused_ref[0])
    def _():
        b = pl.program_id(0)
        prev = bexp_ref[jnp.maximum(b - 1, 0)]

        @pl.when((b == 0) | (bexp_ref[b] != prev))
        def _():
            wg16[...] = wg_ref[0, 0].astype(BF16)
            wu16[...] = wu_ref[0, 0].astype(BF16)
            wd16[...] = wd_ref[0, 0].astype(BF16)

        x_lo, x_hi = _unpack_pairs(x_ref[...])
        x = jnp.concatenate([x_lo.astype(BF16), x_hi.astype(BF16)], axis=1)
        g = jnp.dot(x, wg16[...], preferred_element_type=F32) + bg_ref[0, 0]
        g = jnp.minimum(g, SWIGLU_LIMIT)
        v = jnp.dot(x, wu16[...], preferred_element_type=F32) + bu_ref[0, 0]
        v = jnp.clip(v, -SWIGLU_LIMIT, SWIGLU_LIMIT)
        a = g * jax.nn.sigmoid(SWIGLU_ALPHA * g) * (v + 1.0)
        y = jnp.dot(a.astype(BF16), wd16[...], preferred_element_type=F32) + bd_ref[0, 0]
        o_ref[...] = _pack_pairs(y)


def _expert_call(xs, block_exp, n_used, layer, wg, bg, wu, bu, wd, bd):
    n_slots, dh = xs.shape
    d = 2 * dh
    depth, n_exp, _, f = wg.shape
    te = EXPERT_TILE
    nblk = n_slots // te

    def xmap(b, be, nu):
        return (jnp.minimum(b, nu[0] - 1), 0)

    def wmap(b, be, nu):
        return (layer, be[b], 0, 0)

    return pl.pallas_call(
        _expert_body,
        out_shape=jax.ShapeDtypeStruct((n_slots, dh), U32),
        grid_spec=pltpu.PrefetchScalarGridSpec(
            num_scalar_prefetch=2,
            grid=(nblk,),
            in_specs=[
                pl.BlockSpec((te, dh), xmap),
                pl.BlockSpec((1, 1, d, f), wmap),
                pl.BlockSpec((1, 1, 1, f), wmap),
                pl.BlockSpec((1, 1, d, f), wmap),
                pl.BlockSpec((1, 1, 1, f), wmap),
                pl.BlockSpec((1, 1, f, d), wmap),
                pl.BlockSpec((1, 1, 1, d), wmap),
            ],
            out_specs=pl.BlockSpec((te, dh), xmap),
            scratch_shapes=[pltpu.VMEM((d, f), BF16), pltpu.VMEM((d, f), BF16),
                            pltpu.VMEM((f, d), BF16)],
        ),
        compiler_params=pltpu.CompilerParams(dimension_semantics=("arbitrary",),
                                             vmem_limit_bytes=EXPERT_VMEM_LIMIT),
        name="moe_experts",
    )(block_exp, n_used, xs, wg, bg.reshape(depth, n_exp, 1, f), wu, bu.reshape(depth, n_exp, 1, f),
      wd, bd.reshape(depth, n_exp, 1, d))


def _combine_body(dest_ref, dnext_ref, h_ref, meta_ref, modr_ref, fg_ref, ys_ref, o_ref, buf, sem,
                  *, tc, final):
    i = pl.program_id(0)
    slot = i % 2

    def gather(d_ref, s):
        def issue(j, carry):
            for c in range(SUBLANES):
                base = TOP_K * (j * SUBLANES + c)
                for k in range(TOP_K):
                    pltpu.make_async_copy(ys_ref.at[pl.ds(d_ref[base + k], 1)],
                                          buf.at[s, k, j, pl.ds(c, 1)], sem.at[s]).start()
            return carry

        lax.fori_loop(0, tc // SUBLANES, issue, 0)

    @pl.when(i == 0)
    def _():
        gather(dest_ref, 0)

    @pl.when(i + 1 < pl.num_programs(0))
    def _():
        gather(dnext_ref, 1 - slot)

    _rows_wait(ys_ref, TOP_K * tc, sem.at[slot])

    meta = meta_ref[...]
    acc_lo = acc_hi = None
    for k in range(TOP_K):
        pk = meta[:, 2 * TOP_K + k:2 * TOP_K + k + 1]
        y_lo, y_hi = _unpack_pairs(buf[slot, k].reshape(tc, -1))
        acc_lo = pk * y_lo if k == 0 else acc_lo + pk * y_lo
        acc_hi = pk * y_hi if k == 0 else acc_hi + pk * y_hi
    out = h_ref[...] + modr_ref[0][5:6] * jnp.concatenate([acc_lo, acc_hi], axis=1)
    if final:
        ms = jnp.mean(out * out, axis=-1, keepdims=True)
        out = out * lax.rsqrt(ms + NORM_EPS) * fg_ref[...]
    o_ref[...] = out


def _combine_call(h, meta, modr, final_g, ys, dest, seq, final):
    n, d = h.shape
    tc = min(ROW_TILE, seq)
    tpb = seq // tc
    steps = n // tc
    return pl.pallas_call(
        functools.partial(_combine_body, tc=tc, final=final),
        out_shape=jax.ShapeDtypeStruct((n, d), F32),
        grid=(steps,),
        in_specs=[
            pl.BlockSpec((TOP_K * tc,), lambda i: (i,), memory_space=pltpu.SMEM),
            pl.BlockSpec((TOP_K * tc,), lambda i: (jnp.minimum(i + 1, steps - 1),),
                         memory_space=pltpu.SMEM),
            pl.BlockSpec((tc, d), lambda i: (i, 0)),
            pl.BlockSpec((tc, LANES), lambda i: (i, 0)),
            pl.BlockSpec((1, SUBLANES, d), lambda i: (i // tpb, 0, 0)),
            _resident((1, d)),
            pl.BlockSpec(memory_space=pl.ANY),
        ],
        out_specs=pl.BlockSpec((tc, d), lambda i: (i, 0)),
        scratch_shapes=[pltpu.VMEM((2, TOP_K, tc // SUBLANES, SUBLANES, d // 2), U32),
                        pltpu.SemaphoreType.DMA((2,))],
        compiler_params=pltpu.CompilerParams(dimension_semantics=("arbitrary",),
                                             vmem_limit_bytes=VMEM_LIMIT,
                                             disable_bounds_checks=True),
        name="moe_combine",
    )(dest, dest, h, meta, modr, final_g, ys)


def _moe(h, modr, ng, w_router, b_router, layer, wg, bg, wu, bu, wd, bd, final_g, seq, final):
    n, d = h.shape
    n_exp = w_router.shape[1]
    te = EXPERT_TILE
    n_slots = n * TOP_K + n_exp * te
    nblk = n_slots // te
    u, meta, cnt = _router_call(h, modr, ng, w_router, b_router, seq)

    idx = meta[:, 0:TOP_K].astype(jnp.int32)
    rank = meta[:, TOP_K:2 * TOP_K].astype(jnp.int32)
    counts = cnt[0, :n_exp].astype(jnp.int32)
    padded = (counts + te - 1) // te * te
    pad_ends = jnp.cumsum(padded)
    pad_starts = pad_ends - padded
    dest = (pad_starts[idx] + rank).reshape(-1)
    n_used = pad_ends[-1] // te
    blk = jnp.minimum(jnp.arange(nblk, dtype=jnp.int32), n_used - 1)
    block_exp = jnp.minimum(jnp.sum(pad_ends[None, :] <= (blk * te)[:, None], axis=1),
                            n_exp - 1).astype(jnp.int32)
    within = jnp.arange(te, dtype=jnp.int32)[None, :]
    n_fill = (padded - counts)[:, None]
    fill_rows = jnp.where(within < n_fill, (pad_starts + counts)[:, None] + within, n_slots)
    pad_rows = jnp.sort(fill_rows.reshape(-1)).astype(jnp.int32)
    n_pad = jnp.sum(padded - counts).astype(jnp.int32).reshape(1)

    xs = _dispatch_call(u, dest, pad_rows, n_pad, n_slots)
    ys = _expert_call(xs, block_exp, n_used.astype(jnp.int32).reshape(1), layer,
                      wg, bg, wu, bu, wd, bd)
    return _combine_call(h, meta, modr, final_g, ys, dest, seq, final)


def kernel(x, c, ada_w, ada_b, norm_mix_g, norm_ffn_g, cv_w_in, cv_sc_w, cv_cf_w, cv_cf_b, cv_ln_g, cv_ln_b, cv_w_out, s5_lam_re, s5_lam_im, s5_log_dt, s5_b_re, s5_b_im, s5_c_re, s5_c_im, s5_d, s5_w_glu, router_w, router_b, exp_w_gate, exp_b_gate, exp_w_up, exp_b_up, exp_w_down, exp_b_down, final_g):
    bsz, seq, d = x.shape
    depth = ada_w.shape[0]
    n = bsz * seq
    assert bsz <= SUBLANES
    h = x.reshape(n, d)
    mod = _mod_call(c, ada_w, ada_b)
    fg = final_g.reshape(1, d)
    for layer in range(depth):
        modr = jnp.zeros((bsz, SUBLANES, d), F32).at[:, :6].set(mod[layer, :bsz].reshape(bsz, 6, d))
        ng = norm_mix_g[layer].reshape(1, d)
        j = layer // 2
        if layer % 2 == 0:
            h = _conv_mixer_call(h, modr, ng, cv_w_in[j], cv_sc_w[j], cv_cf_w[j], cv_cf_b[j],
                                 cv_ln_g[j], cv_ln_b[j], cv_w_out[j], bsz, seq)
        else:
            h = _s5_mixer(h, modr, ng, s5_lam_re[j], s5_lam_im[j], s5_log_dt[j], s5_b_re[j],
                          s5_b_im[j], s5_c_re[j], s5_c_im[j], s5_d[j], s5_w_glu[j], bsz, seq)
        h = _moe(h, modr, norm_ffn_g[layer].reshape(1, d), router_w[layer], router_b[layer],
                 layer, exp_w_gate, exp_b_gate, exp_w_up, exp_b_up, exp_w_down, exp_b_down, fg, seq,
                 layer == depth - 1)
    return h.reshape(bsz, seq, d)
```

```python
import functools

import jax
import jax.numpy as jnp
from jax import lax
from jax.experimental import pallas as pl
from jax.experimental.pallas import tpu as pltpu

F32 = jnp.float32
BF16 = jnp.bfloat16
NORM_EPS = 1e-5
TOP_K = 4
SWIGLU_LIMIT = 7.0
SWIGLU_ALPHA = 1.702
LANES = 128
SUBLANES = 8
VMEM_LIMIT = 56 * 1024 * 1024
EXPERT_VMEM_LIMIT = 60 * 1024 * 1024
S5_CHUNK = 16
ROW_TILE = 256
EXPERT_TILE = 256
DMA_UNROLL = 8
NEG_BIG = -3.0e38


def _cparams(sem):
    return pltpu.CompilerParams(dimension_semantics=sem, vmem_limit_bytes=VMEM_LIMIT)


def _largest_divisor(n, candidates):
    for c in candidates:
        if n % c == 0:
            return c
    return n


def _norm_mod(x, g, shift, scale):
    ms = jnp.mean(x * x, axis=-1, keepdims=True)
    return x * lax.rsqrt(ms + NORM_EPS) * g * (1.0 + scale) + shift


def _resident(shape):
    nd = len(shape)
    return pl.BlockSpec(shape, lambda *_: (0,) * nd, pipeline_mode=pl.Buffered(1))


def _mod_body(c_ref, w_ref, b_ref, o_ref):
    c = c_ref[...]
    ca = (c * jax.nn.sigmoid(c)).astype(BF16)
    o_ref[0] = jnp.dot(ca, w_ref[0].astype(BF16), preferred_element_type=F32) + b_ref[0]


def _mod_call(c, ada_w, ada_b):
    depth, d, w = ada_w.shape
    b = c.shape[0]
    c8 = jnp.zeros((SUBLANES, d), F32).at[:b].set(c)
    tn = _largest_divisor(w, (1536, 1024, 768, 512, 384, 256, 128))
    return pl.pallas_call(
        _mod_body,
        out_shape=jax.ShapeDtypeStruct((depth, SUBLANES, w), F32),
        grid=(depth, w // tn),
        in_specs=[
            pl.BlockSpec((SUBLANES, d), lambda l, j: (0, 0)),
            pl.BlockSpec((1, d, tn), lambda l, j: (l, 0, j)),
            pl.BlockSpec((1, 1, tn), lambda l, j: (l, 0, j)),
        ],
        out_specs=pl.BlockSpec((1, SUBLANES, tn), lambda l, j: (l, 0, j)),
        compiler_params=_cparams(("arbitrary", "arbitrary")),
        name="adaln_mod",
    )(c8, ada_w, ada_b.reshape(depth, 1, w))


def _conv_body(h_ref, modr_ref, ng_ref, win_ref, scw_ref, cfw_ref, cfv_ref, wout_ref, o_ref,
               gbuf, cvbuf, ycat, wbc, *, tm, sc, cf, ka, kb, ha, hb, rc):
    @pl.when((pl.program_id(0) == 0) & (pl.program_id(1) == 0))
    def _():
        for k in range(kb):
            wbc[k] = jnp.broadcast_to(cfw_ref[k:k + 1, :], (SUBLANES, cf))

    @pl.when(pl.program_id(1) == 0)
    def _():
        gbuf[0, 0:hb, :] = jnp.zeros((hb, cf), F32)
        cvbuf[0:ha, :] = jnp.zeros((ha, sc), F32)

    x = h_ref[...]
    m = modr_ref[0]
    u16 = _norm_mod(x, ng_ref[...], m[0:1], m[1:2]).astype(BF16)

    zb = jnp.dot(u16, win_ref[:, 3 * sc:3 * sc + 2 * cf], preferred_element_type=F32)
    gbuf[0, hb:hb + tm, :] = zb[:, 0:cf] * jax.nn.sigmoid(zb[:, cf:2 * cf])
    span = tm + hb - SUBLANES
    for b in range(1, SUBLANES):
        for c0 in range(0, span, 64):
            c1 = min(c0 + 64, span)
            gbuf[b, c0:c1, :] = gbuf[0, b + c0:b + c1, :]

    za = jnp.dot(u16, win_ref[:, 0:3 * sc], preferred_element_type=F32)
    cv = za[:, sc:2 * sc] * za[:, 2 * sc:3 * sc]
    cvbuf[ha:ha + tm, :] = cv
    acc = scw_ref[ka - 1:ka, :] * cv
    for k in range(ka - 1):
        off = ha - (ka - 1) + k
        acc = acc + scw_ref[k:k + 1, :] * cvbuf[off:off + tm, :]
    ycat[:, 0:sc] = (za[:, 0:sc] * acc).astype(BF16)
    y = jnp.dot(ycat[:, 0:sc], wout_ref[0:sc, :], preferred_element_type=F32)

    cf_b = cfv_ref[0:1, :]
    ln_g = cfv_ref[1:2, :]
    ln_b = cfv_ref[2:3, :]
    for r0 in range(0, tm, rc):
        rows = range(r0, r0 + rc, SUBLANES)
        w_last = wbc[kb - 1]
        accs = [w_last * gbuf[0, hb + r:hb + r + SUBLANES, :] for r in rows]
        for k in range(kb - 1):
            off = hb - (kb - 1) + k
            al = off // SUBLANES * SUBLANES
            wk = wbc[k]
            for c, r in enumerate(rows):
                accs[c] = accs[c] + wk * gbuf[off % SUBLANES, al + r:al + r + SUBLANES, :]
        a = jnp.concatenate(accs, axis=0) + cf_b
        mu = jnp.mean(a, axis=-1, keepdims=True)
        ac = a - mu
        var = jnp.mean(ac * ac, axis=-1, keepdims=True)
        yn = ac * lax.rsqrt(var + NORM_EPS) * ln_g + ln_b
        ycat[r0:r0 + rc, sc:sc + cf] = (yn * jax.nn.sigmoid(yn)).astype(BF16)

    cvbuf[0:ha, :] = cvbuf[tm:tm + ha, :]
    gbuf[0, 0:hb, :] = gbuf[0, tm:tm + hb, :]

    y = y + jnp.dot(ycat[:, sc:sc + cf], wout_ref[sc:sc + cf, :], preferred_element_type=F32)
    o_ref[...] = x + m[2:3] * y


def _conv_mixer_call(h, modr, ng, w_in, sc_w, cf_w, cf_b, ln_g, ln_b, w_out, bsz, seq):
    n, d = h.shape
    ka, sc = sc_w.shape
    kb, cf = cf_w.shape
    ha = -(-(ka - 1) // SUBLANES) * SUBLANES
    hb = -(-(kb - 1) // SUBLANES) * SUBLANES
    tm = min(ROW_TILE, seq)
    assert seq % tm == 0 and tm >= hb and tm >= ha
    ns = seq // tm
    scw = jnp.zeros((SUBLANES, sc), F32).at[:ka].set(sc_w)
    cfw = jnp.zeros((-(-kb // SUBLANES) * SUBLANES, cf), F32).at[:kb].set(cf_w)
    cfv = jnp.zeros((SUBLANES, cf), F32).at[0].set(cf_b).at[1].set(ln_g).at[2].set(ln_b)
    body = functools.partial(_conv_body, tm=tm, sc=sc, cf=cf, ka=ka, kb=kb, ha=ha, hb=hb,
                             rc=min(16, tm))
    return pl.pallas_call(
        body,
        out_shape=jax.ShapeDtypeStruct((n, d), F32),
        grid=(bsz, ns),
        in_specs=[
            pl.BlockSpec((tm, d), lambda b, s: (b * ns + s, 0)),
            pl.BlockSpec((1, SUBLANES, d), lambda b, s: (b, 0, 0)),
            _resident((1, d)),
            _resident(w_in.shape),
            _resident(scw.shape),
            _resident(cfw.shape),
            _resident(cfv.shape),
            _resident(w_out.shape),
        ],
        out_specs=pl.BlockSpec((tm, d), lambda b, s: (b * ns + s, 0)),
        scratch_shapes=[
            pltpu.VMEM((SUBLANES, tm + hb, cf), F32),
            pltpu.VMEM((tm + ha, sc), F32),
            pltpu.VMEM((tm, sc + cf), BF16),
            pltpu.VMEM((cfw.shape[0], SUBLANES, cf), F32),
        ],
        compiler_params=_cparams(("arbitrary", "arbitrary")),
        name="conv_mixer",
    )(h, modr, ng, w_in.astype(BF16), scw, cfw, cfv, w_out.astype(BF16))


def _norm_body(h_ref, modr_ref, ng_ref, o_ref):
    m = modr_ref[0]
    o_ref[...] = _norm_mod(h_ref[...], ng_ref[...], m[0:1], m[1:2]).astype(BF16)


def _norm_call(h, modr, ng, seq):
    n, d = h.shape
    tm = min(2 * ROW_TILE, seq)
    tpb = seq // tm
    return pl.pallas_call(
        _norm_body,
        out_shape=jax.ShapeDtypeStruct((n, d), BF16),
        grid=(n // tm,),
        in_specs=[
            pl.BlockSpec((tm, d), lambda i: (i, 0)),
            pl.BlockSpec((1, SUBLANES, d), lambda i: (i // tpb, 0, 0)),
            _resident((1, d)),
        ],
        out_specs=pl.BlockSpec((tm, d), lambda i: (i, 0)),
        compiler_params=_cparams(("arbitrary",)),
        name="s5_norm",
    )(h, modr, ng)


def _s5_prep(lam_re, lam_im, log_dt, b_re, b_im, c_re, c_im, chunk):
    hi = lax.Precision.HIGHEST
    dt = jnp.exp(log_dt)[:, None]
    mag = jnp.exp(lam_re * dt)
    lb_re = mag * jnp.cos(lam_im * dt)
    lb_im = mag * jnp.sin(lam_im * dt)
    den = lam_re * lam_re + lam_im * lam_im
    f_re = ((lb_re - 1) * lam_re + lb_im * lam_im) / den
    f_im = (lb_im * lam_re - (lb_re - 1) * lam_im) / den
    bb_re = f_re[..., None] * b_re - f_im[..., None] * b_im
    bb_im = f_re[..., None] * b_im + f_im[..., None] * b_re
    g, p, hh = bb_re.shape
    prs, pis = [jnp.ones_like(lb_re)], [jnp.zeros_like(lb_im)]
    for _ in range(chunk):
        pr, pi = prs[-1], pis[-1]
        prs.append(pr * lb_re - pi * lb_im)
        pis.append(pr * lb_im + pi * lb_re)
    pr = jnp.stack(prs)
    pi = jnp.stack(pis)
    kmat = (jnp.einsum('ghp,tgp,gpk->tghk', c_re, pr[:chunk], bb_re, precision=hi)
            - jnp.einsum('ghp,tgp,gpk->tghk', c_re, pi[:chunk], bb_im, precision=hi)
            - jnp.einsum('ghp,tgp,gpk->tghk', c_im, pr[:chunk], bb_im, precision=hi)
            - jnp.einsum('ghp,tgp,gpk->tghk', c_im, pi[:chunk], bb_re, precision=hi))
    prr = pr[chunk - 1::-1][:chunk]
    pir = pi[chunk - 1::-1][:chunk]
    w_re = prr[..., None] * bb_re[None] - pir[..., None] * bb_im[None]
    w_im = prr[..., None] * bb_im[None] + pir[..., None] * bb_re[None]
    w_in = jnp.concatenate([w_re.transpose(1, 0, 3, 2), w_im.transpose(1, 0, 3, 2)], axis=-1)
    w_in = w_in.reshape(g, chunk * hh, 2 * p)
    p1r = pr[1:chunk + 1][:, :, None, :]
    p1i = pi[1:chunk + 1][:, :, None, :]
    co_r = c_re[None] * p1r - c_im[None] * p1i
    co_i = -c_re[None] * p1i - c_im[None] * p1r
    w_out = jnp.concatenate([co_r.transpose(1, 3, 0, 2), co_i.transpose(1, 3, 0, 2)], axis=1)
    w_out = w_out.reshape(g, 2 * p, chunk * hh)
    return kmat, w_in, w_out, pr[chunk], pi[chunk]


def _s5_compact_blocks(kmat, w_in, w_out, chunk, hh, p):
    g = kmat.shape[1]
    gpb = LANES // hh
    nlb = g // gpb
    eye = jnp.eye(gpb, dtype=F32)
    k6 = kmat.reshape(chunk, nlb, gpb, hh, hh).transpose(1, 0, 2, 4, 3)
    bd = (k6[:, :, :, :, None, :] * eye[None, None, :, None, :, None]).reshape(
        nlb, chunk, LANES, LANES)
    wi = w_in.reshape(nlb, gpb, chunk, hh, 2, p).transpose(0, 2, 4, 1, 3, 5)
    wi = wi.reshape(nlb, chunk, 2, LANES, p)
    wi_c = jnp.concatenate([wi] * (LANES // p), axis=-1)
    wo_c = w_out.reshape(nlb, gpb, 2, p, chunk * hh).transpose(0, 2, 1, 3, 4)
    t_i = jnp.arange(chunk)
    h_i = jnp.arange(hh)
    rep = ((t_i[:, None, None, None, None] == t_i[None, None, :, None, None])
           & (h_i[None, :, None, None, None] == h_i[None, None, None, None, :]))
    rep = jnp.broadcast_to(rep, (chunk, hh, chunk, gpb, hh)).reshape(chunk * hh, chunk * LANES)
    return bd.astype(BF16), wi_c.astype(BF16), wo_c.astype(BF16), rep.astype(BF16)


def _fold_chunks(u_ref, ubuf, ucat, chunk):
    ubuf[...] = u_ref[...].astype(F32)
    for s in range(chunk):
        ucat[:, s * LANES:(s + 1) * LANES] = ubuf[:, s, :].astype(BF16)


def _s5_state_body(u_ref, wic_ref, ore_ref, oim_ref, uc_ref, ubuf, ucat, wvm, *, chunk, half, hh, p):
    @pl.when(pl.program_id(1) == 0)
    def _():
        row_g = lax.broadcasted_iota(jnp.int32, (LANES, LANES), 0) // hh
        lane_g = lax.broadcasted_iota(jnp.int32, (LANES, LANES), 1) // p
        for c in range(half // LANES):
            mask = row_g == (LANES // p) * c + lane_g
            for s in range(chunk):
                for q in range(2):
                    lo = q * half + c * LANES
                    wvm[s * LANES:(s + 1) * LANES, lo:lo + LANES] = jnp.where(
                        mask, wic_ref[0, s, q], jnp.zeros((LANES, LANES), BF16))

    _fold_chunks(u_ref, ubuf, ucat, chunk)
    uc_ref[0] = ucat[...]
    s = jnp.dot(ucat[...], wvm[...], preferred_element_type=F32)
    ore_ref[...] = s[:, 0:half]
    oim_ref[...] = s[:, half:2 * half]


def _s5_state_call(u3, wi_c, hh, p):
    nc, chunk, d = u3.shape
    nlb = d // LANES
    half = LANES // hh * p
    gp = nlb * half
    wide = chunk * LANES
    cb = _largest_divisor(nc, (256, 128, 64, 32, 16, 8))
    return pl.pallas_call(
        functools.partial(_s5_state_body, chunk=chunk, half=half, hh=hh, p=p),
        out_shape=(jax.ShapeDtypeStruct((nc, gp), F32), jax.ShapeDtypeStruct((nc, gp), F32),
                   jax.ShapeDtypeStruct((nlb, nc, wide), BF16)),
        grid=(nlb, nc // cb),
        in_specs=[
            pl.BlockSpec((cb, chunk, LANES), lambda j, i: (i, 0, j)),
            pl.BlockSpec((1, chunk, 2, LANES, LANES), lambda j, i: (j, 0, 0, 0, 0)),
        ],
        out_specs=(pl.BlockSpec((cb, half), lambda j, i: (i, j)),
                   pl.BlockSpec((cb, half), lambda j, i: (i, j)),
                   pl.BlockSpec((1, cb, wide), lambda j, i: (j, i, 0))),
        scratch_shapes=[pltpu.VMEM((cb, chunk, LANES), F32), pltpu.VMEM((cb, chunk * LANES), BF16),
                        pltpu.VMEM((chunk * LANES, 2 * half), BF16)],
        compiler_params=_cparams(("arbitrary", "arbitrary")),
        name="s5_chunk_state",
    )(u3, wi_c)


def _s5_scan_body(sre_ref, sim_ref, a_ref, ore_ref, oim_ref, carry, *, cb):
    @pl.when(pl.program_id(1) == 0)
    def _():
        carry[...] = jnp.zeros_like(carry)

    ar = a_ref[0]
    ai = a_ref[1]

    def step(c, st):
        xr, xi = st
        ore_ref[c] = xr.astype(ore_ref.dtype)
        oim_ref[c] = xi.astype(oim_ref.dtype)
        return (ar * xr - ai * xi + sre_ref[c], ar * xi + ai * xr + sim_ref[c])

    xr, xi = lax.fori_loop(0, cb, step, (carry[0], carry[1]))
    carry[0] = xr
    carry[1] = xi


def _s5_scan_call(s_re, s_im, a2, bsz):
    nc, r, _ = s_re.shape
    ncb = nc // bsz
    cb = _largest_divisor(ncb, (64, 32, 16, 8, 4, 2, 1))
    nb = ncb // cb
    blk = pl.BlockSpec((cb, r, LANES), lambda b, j: (b * nb + j, 0, 0))
    return pl.pallas_call(
        functools.partial(_s5_scan_body, cb=cb),
        out_shape=(jax.ShapeDtypeStruct(s_re.shape, BF16), jax.ShapeDtypeStruct(s_re.shape, BF16)),
        grid=(bsz, nb),
        in_specs=[blk, blk, _resident(a2.shape)],
        out_specs=(blk, blk),
        scratch_shapes=[pltpu.VMEM((2, r, LANES), F32)],
        compiler_params=_cparams(("arbitrary", "arbitrary")),
        name="s5_state_scan",
    )(s_re, s_im, a2)


def _s5_out_body(uc_ref, bd_ref, xre_ref, xim_ref, woc_ref, rep_ref, o_ref, ybuf, tvm, wovm,
                 *, chunk, half, hh, p):
    @pl.when(pl.program_id(1) == 0)
    def _():
        zero = jnp.zeros((LANES, LANES), BF16)
        for s in range(chunk):
            for t in range(chunk):
                tvm[s * LANES:(s + 1) * LANES, t * LANES:(t + 1) * LANES] = (
                    bd_ref[0, t - s] if t >= s else zero)
        lane_g = lax.broadcasted_iota(jnp.int32, (p, chunk * LANES), 1) % LANES // hh
        gpb = LANES // hh
        for q in range(2):
            for a in range(gpb):
                spread = jnp.dot(woc_ref[0, q, a], rep_ref[...], preferred_element_type=F32)
                wovm[(q * gpb + a) * p:(q * gpb + a + 1) * p, :] = jnp.where(
                    lane_g == a, spread, 0.0).astype(BF16)

    y = jnp.dot(uc_ref[0], tvm[...], preferred_element_type=F32)
    y = y + jnp.dot(xre_ref[...], wovm[0:half, :], preferred_element_type=F32)
    y = y + jnp.dot(xim_ref[...], wovm[half:2 * half, :], preferred_element_type=F32)
    for t in range(chunk):
        ybuf[:, t, :] = y[:, t * LANES:(t + 1) * LANES]
    o_ref[...] = ybuf[...].astype(BF16)


def _s5_out_call(ucat, bd, xs_re, xs_im, wo_c, rep, hh, p):
    nlb, nc, wide = ucat.shape
    chunk = wide // LANES
    d = nlb * LANES
    gpb = LANES // hh
    half = gpb * p
    cb = _largest_divisor(nc, (256, 128, 64, 32, 16, 8))
    return pl.pallas_call(
        functools.partial(_s5_out_body, chunk=chunk, half=half, hh=hh, p=p),
        out_shape=jax.ShapeDtypeStruct((nc, chunk, d), BF16),
        grid=(nlb, nc // cb),
        in_specs=[
            pl.BlockSpec((1, cb, wide), lambda j, i: (j, i, 0)),
            pl.BlockSpec((1, chunk, LANES, LANES), lambda j, i: (j, 0, 0, 0)),
            pl.BlockSpec((cb, half), lambda j, i: (i, j)),
            pl.BlockSpec((cb, half), lambda j, i: (i, j)),
            pl.BlockSpec((1, 2, gpb, p, chunk * hh), lambda j, i: (j, 0, 0, 0, 0)),
            _resident(rep.shape),
        ],
        out_specs=pl.BlockSpec((cb, chunk, LANES), lambda j, i: (i, 0, j)),
        scratch_shapes=[pltpu.VMEM((cb, chunk, LANES), F32), pltpu.VMEM((wide, wide), BF16),
                        pltpu.VMEM((2 * half, wide), BF16)],
        compiler_params=_cparams(("arbitrary", "arbitrary")),
        name="s5_chunk_out",
    )(ucat, bd, xs_re, xs_im, wo_c, rep)


def _s5_post_body(h_ref, y_ref, modr_ref, ng_ref, dsk_ref, wglu_ref, o_ref, *, d):
    x = h_ref[...]
    m = modr_ref[0]
    u = _norm_mod(x, ng_ref[...], m[0:1], m[1:2])
    yy = jax.nn.gelu(y_ref[...].astype(F32) + dsk_ref[...] * u)
    r = jnp.dot(yy.astype(BF16), wglu_ref[...], preferred_element_type=F32)
    o_ref[...] = x + m[2:3] * (r[:, 0:d] * jax.nn.sigmoid(r[:, d:2 * d]))


def _s5_post_call(h, y, modr, ng, d_skip, w_glu, seq):
    n, d = h.shape
    tm = min(ROW_TILE, seq)
    tpb = seq // tm
    return pl.pallas_call(
        functools.partial(_s5_post_body, d=d),
        out_shape=jax.ShapeDtypeStruct((n, d), F32),
        grid=(n // tm,),
        in_specs=[
            pl.BlockSpec((tm, d), lambda i: (i, 0)),
            pl.BlockSpec((tm, d), lambda i: (i, 0)),
            pl.BlockSpec((1, SUBLANES, d), lambda i: (i // tpb, 0, 0)),
            _resident((1, d)),
            _resident((1, d)),
            _resident(w_glu.shape),
        ],
        out_specs=pl.BlockSpec((tm, d), lambda i: (i, 0)),
        compiler_params=_cparams(("arbitrary",)),
        name="s5_glu",
    )(h, y, modr, ng, d_skip, w_glu.astype(BF16))


def _s5_mixer(h, modr, ng, lam_re, lam_im, log_dt, b_re, b_im, c_re, c_im, d_skip, w_glu,
              bsz, seq):
    n, d = h.shape
    g, p, hh = b_re.shape
    chunk = min(S5_CHUNK, seq)
    nc = n // chunk
    kmat, w_in, w_out, a_re, a_im = _s5_prep(lam_re, lam_im, log_dt, b_re, b_im, c_re, c_im, chunk)
    bd, wi_c, wo_c, rep = _s5_compact_blocks(kmat, w_in, w_out, chunk, hh, p)
    u3 = _norm_call(h, modr, ng, seq).reshape(nc, chunk, d)
    s_re, s_im, ucat = _s5_state_call(u3, wi_c, hh, p)
    r = g * p // LANES
    a2 = jnp.stack([a_re, a_im]).reshape(2, r, LANES)
    xs_re, xs_im = _s5_scan_call(s_re.reshape(nc, r, LANES), s_im.reshape(nc, r, LANES), a2, bsz)
    y = _s5_out_call(ucat, bd, xs_re.reshape(nc, g * p), xs_im.reshape(nc, g * p), wo_c, rep,
                     hh, p).reshape(n, d)
    return _s5_post_call(h, y, modr, ng, d_skip.reshape(1, d), w_glu, seq)


U32 = jnp.uint32
HI_MASK = 0xFFFF0000


def _pack_pairs(x):
    half = x.shape[1] // 2
    lo = lax.bitcast_convert_type(x[:, :half].astype(BF16).astype(F32), U32)
    hi = lax.bitcast_convert_type(x[:, half:].astype(BF16).astype(F32), U32)
    return (lo >> 16) | (hi & U32(HI_MASK))


def _unpack_pairs(w):
    lo = lax.bitcast_convert_type(w << 16, F32)
    hi = lax.bitcast_convert_type(w & U32(HI_MASK), F32)
    return lo, hi


def _router_body(h_ref, modr_ref, ng_ref, whi_ref, wlo_ref, rb_ref, u_ref, meta_ref, cnt_ref,
                 carry, *, n_exp, tm):
    @pl.when(pl.program_id(0) == 0)
    def _():
        carry[...] = jnp.zeros_like(carry)

    m = modr_ref[0]
    u = _norm_mod(h_ref[...], ng_ref[...], m[3:4], m[4:5])
    u_ref[...] = _pack_pairs(u)
    uh = u.astype(BF16)
    ul = (u - uh.astype(F32)).astype(BF16)
    whi = whi_ref[...]
    logits = (jnp.dot(uh, whi, preferred_element_type=F32)
              + jnp.dot(ul, whi, preferred_element_type=F32)
              + jnp.dot(uh, wlo_ref[...], preferred_element_type=F32)) + rb_ref[...]
    lane = lax.broadcasted_iota(jnp.int32, (tm, LANES), 1)
    lane_f = lane.astype(F32)
    l = jnp.where(lane < n_exp, logits, NEG_BIG)
    tops, idxs, sels = [], [], []
    for _ in range(TOP_K):
        mk = jnp.max(l, axis=-1, keepdims=True)
        ik = jnp.min(jnp.where(l == mk, lane_f, 1.0e9), axis=-1, keepdims=True)
        sk = lane_f == ik
        l = jnp.where(sk, NEG_BIG, l)
        tops.append(mk)
        idxs.append(ik)
        sels.append(sk)
    es = [jnp.exp(t - tops[0]) for t in tops]
    den = es[0]
    for e in es[1:]:
        den = den + e
    chosen = jnp.zeros((tm, LANES), F32)
    for sk in sels:
        chosen = jnp.where(sk, 1.0, chosen)
    row = lax.broadcasted_iota(jnp.int32, (tm, tm), 0)
    col = lax.broadcasted_iota(jnp.int32, (tm, tm), 1)
    tri = jnp.where(row > col, 1.0, 0.0).astype(BF16)
    rank_mat = jnp.dot(tri, chosen.astype(BF16), preferred_element_type=F32) + carry[0:1, :]
    meta = jnp.zeros((tm, LANES), F32)
    for k in range(TOP_K):
        rk = jnp.sum(jnp.where(sels[k], rank_mat, 0.0), axis=-1, keepdims=True)
        meta = jnp.where(lane == k, idxs[k], meta)
        meta = jnp.where(lane == TOP_K + k, rk, meta)
        meta = jnp.where(lane == 2 * TOP_K + k, es[k] / den, meta)
    meta_ref[...] = meta
    carry[...] = carry[...] + jnp.sum(chosen, axis=0, keepdims=True)
    cnt_ref[...] = carry[...]


def _router_call(h, modr, ng, w_router, b_router, seq):
    n, d = h.shape
    n_exp = w_router.shape[1]
    tm = min(ROW_TILE, seq)
    tpb = seq // tm
    wpad = jnp.zeros((d, LANES), F32).at[:, :n_exp].set(w_router)
    whi = wpad.astype(BF16)
    wlo = (wpad - whi.astype(F32)).astype(BF16)
    rb = jnp.zeros((1, LANES), F32).at[0, :n_exp].set(b_router)
    return pl.pallas_call(
        functools.partial(_router_body, n_exp=n_exp, tm=tm),
        out_shape=(
            jax.ShapeDtypeStruct((n, d // 2), U32),
            jax.ShapeDtypeStruct((n, LANES), F32),
            jax.ShapeDtypeStruct((SUBLANES, LANES), F32),
        ),
        grid=(n // tm,),
        in_specs=[
            pl.BlockSpec((tm, d), lambda i: (i, 0)),
            pl.BlockSpec((1, SUBLANES, d), lambda i: (i // tpb, 0, 0)),
            _resident((1, d)),
            _resident(whi.shape),
            _resident(wlo.shape),
            _resident(rb.shape),
        ],
        out_specs=(
            pl.BlockSpec((tm, d // 2), lambda i: (i, 0)),
            pl.BlockSpec((tm, LANES), lambda i: (i, 0)),
            pl.BlockSpec((SUBLANES, LANES), lambda i: (0, 0)),
        ),
        scratch_shapes=[pltpu.VMEM((SUBLANES, LANES), F32)],
        compiler_params=_cparams(("arbitrary",)),
        name="moe_router",
    )(h, modr, ng, whi, wlo, rb)


def _row_copy(src_ref, src_row, dst_ref, dst_row, sem):
    return pltpu.make_async_copy(src_ref.at[pl.ds(src_row, 1)], dst_ref.at[pl.ds(dst_row, 1)], sem)


def _rows_wait(hbm_ref, rows, sem):
    pltpu.make_async_copy(hbm_ref.at[pl.ds(0, rows)], hbm_ref.at[pl.ds(0, rows)], sem).wait()


def _dispatch_body(npad_ref, padrow_ref, dest_ref, u_ref, xs_ref, zrow, sem, *, td, pp):
    i = pl.program_id(0)
    zrow[...] = jnp.zeros_like(zrow)
    n_pad = jnp.clip(npad_ref[0] - i * pp, 0, pp)

    def issue(r, c):
        for k in range(TOP_K):
            _row_copy(u_ref, r, xs_ref, dest_ref[TOP_K * r + k], sem).start(priority=k % 2)
        return c

    lax.fori_loop(0, td, issue, 0, unroll=DMA_UNROLL)

    def issue_pad(j, c):
        _row_copy(zrow, 0, xs_ref, padrow_ref[i * pp + j], sem).start()
        return c

    lax.fori_loop(0, n_pad, issue_pad, 0)

    _rows_wait(xs_ref, TOP_K * td, sem)

    def wait_one(j, c):
        _row_copy(zrow, 0, xs_ref, 0, sem).wait()
        return c

    lax.fori_loop(0, n_pad, wait_one, 0)


def _dispatch_call(u, dest, pad_rows, n_pad, n_slots):
    n, d = u.shape
    td = min(2 * ROW_TILE, n)
    steps = n // td
    pp = -(-pad_rows.shape[0] // steps)
    return pl.pallas_call(
        functools.partial(_dispatch_body, td=td, pp=pp),
        out_shape=jax.ShapeDtypeStruct((n_slots, d), u.dtype),
        grid_spec=pltpu.PrefetchScalarGridSpec(
            num_scalar_prefetch=2,
            grid=(steps,),
            in_specs=[
                pl.BlockSpec((TOP_K * td,), lambda i, npad, prow: (i,), memory_space=pltpu.SMEM),
                pl.BlockSpec((td, d), lambda i, npad, prow: (i, 0)),
            ],
            out_specs=pl.BlockSpec(memory_space=pl.ANY),
            scratch_shapes=[pltpu.VMEM((SUBLANES, d), u.dtype), pltpu.SemaphoreType.DMA(())],
        ),
        compiler_params=pltpu.CompilerParams(dimension_semantics=("arbitrary",),
                                             vmem_limit_bytes=VMEM_LIMIT, has_side_effects=True,
                                             disable_bounds_checks=True),
        name="moe_dispatch",
    )(n_pad, pad_rows, dest, u)


def _expert_body(bexp_ref, nused_ref, x_ref, wg_ref, bg_ref, wu_ref, bu_ref, wd_ref, bd_ref, o_ref,
                 wg16, wu16, wd16):
    @pl.when(pl.program_id(0) < nused_ref[0])
    def _():
        b = pl.program_id(0)
        prev = bexp_ref[jnp.maximum(b - 1, 0)]

        @pl.when((b == 0) | (bexp_ref[b] != prev))
        def _():
            wg16[...] = wg_ref[0, 0].astype(BF16)
            wu16[...] = wu_ref[0, 0].astype(BF16)
            wd16[...] = wd_ref[0, 0].astype(BF16)

        x_lo, x_hi = _unpack_pairs(x_ref[...])
        x = jnp.concatenate([x_lo.astype(BF16), x_hi.astype(BF16)], axis=1)
        g = jnp.dot(x, wg16[...], preferred_element_type=F32) + bg_ref[0, 0]
        g = jnp.minimum(g, SWIGLU_LIMIT)
        v = jnp.dot(x, wu16[...], preferred_element_type=F32) + bu_ref[0, 0]
        v = jnp.clip(v, -SWIGLU_LIMIT, SWIGLU_LIMIT)
        a = g * jax.nn.sigmoid(SWIGLU_ALPHA * g) * (v + 1.0)
        y = jnp.dot(a.astype(BF16), wd16[...], preferred_element_type=F32) + bd_ref[0, 0]
        o_ref[...] = _pack_pairs(y)


def _expert_call(xs, block_exp, n_used, layer, wg, bg, wu, bu, wd, bd):
    n_slots, dh = xs.shape
    d = 2 * dh
    depth, n_exp, _, f = wg.shape
    te = EXPERT_TILE
    nblk = n_slots // te

    def xmap(b, be, nu):
        return (jnp.minimum(b, nu[0] - 1), 0)

    def wmap(b, be, nu):
        return (layer, be[b], 0, 0)

    return pl.pallas_call(
        _expert_body,
        out_shape=jax.ShapeDtypeStruct((n_slots, dh), U32),
        grid_spec=pltpu.PrefetchScalarGridSpec(
            num_scalar_prefetch=2,
            grid=(nblk,),
            in_specs=[
                pl.BlockSpec((te, dh), xmap),
                pl.BlockSpec((1, 1, d, f), wmap),
                pl.BlockSpec((1, 1, 1, f), wmap),
                pl.BlockSpec((1, 1, d, f), wmap),
                pl.BlockSpec((1, 1, 1, f), wmap),
                pl.BlockSpec((1, 1, f, d), wmap),
                pl.BlockSpec((1, 1, 1, d), wmap),
            ],
            out_specs=pl.BlockSpec((te, dh), xmap),
            scratch_shapes=[pltpu.VMEM((d, f), BF16), pltpu.VMEM((d, f), BF16),
                            pltpu.VMEM((f, d), BF16)],
        ),
        compiler_params=pltpu.CompilerParams(dimension_semantics=("arbitrary",),
                                             vmem_limit_bytes=EXPERT_VMEM_LIMIT),
        name="moe_experts",
    )(block_exp, n_used, xs, wg, bg.reshape(depth, n_exp, 1, f), wu, bu.reshape(depth, n_exp, 1, f),
      wd, bd.reshape(depth, n_exp, 1, d))


def _combine_body(dest_ref, dnext_ref, h_ref, meta_ref, modr_ref, fg_ref, ys_ref, o_ref, buf, sem,
                  *, tc, final):
    i = pl.program_id(0)
    slot = i % 2

    def gather(d_ref, s):
        def issue(j, carry):
            for c in range(SUBLANES):
                base = TOP_K * (j * SUBLANES + c)
                for k in range(TOP_K):
                    pltpu.make_async_copy(ys_ref.at[pl.ds(d_ref[base + k], 1)],
                                          buf.at[s, k, j, pl.ds(c, 1)], sem.at[s]).start(
                                              priority=k % 2)
            return carry

        lax.fori_loop(0, tc // SUBLANES, issue, 0)

    @pl.when(i == 0)
    def _():
        gather(dest_ref, 0)

    @pl.when(i + 1 < pl.num_programs(0))
    def _():
        gather(dnext_ref, 1 - slot)

    _rows_wait(ys_ref, TOP_K * tc, sem.at[slot])

    meta = meta_ref[...]
    acc_lo = acc_hi = None
    for k in range(TOP_K):
        pk = meta[:, 2 * TOP_K + k:2 * TOP_K + k + 1]
        y_lo, y_hi = _unpack_pairs(buf[slot, k].reshape(tc, -1))
        acc_lo = pk * y_lo if k == 0 else acc_lo + pk * y_lo
        acc_hi = pk * y_hi if k == 0 else acc_hi + pk * y_hi
    out = h_ref[...] + modr_ref[0][5:6] * jnp.concatenate([acc_lo, acc_hi], axis=1)
    if final:
        ms = jnp.mean(out * out, axis=-1, keepdims=True)
        out = out * lax.rsqrt(ms + NORM_EPS) * fg_ref[...]
    o_ref[...] = out


def _combine_call(h, meta, modr, final_g, ys, dest, seq, final):
    n, d = h.shape
    tc = min(ROW_TILE, seq)
    tpb = seq // tc
    steps = n // tc
    return pl.pallas_call(
        functools.partial(_combine_body, tc=tc, final=final),
        out_shape=jax.ShapeDtypeStruct((n, d), F32),
        grid=(steps,),
        in_specs=[
            pl.BlockSpec((TOP_K * tc,), lambda i: (i,), memory_space=pltpu.SMEM),
            pl.BlockSpec((TOP_K * tc,), lambda i: (jnp.minimum(i + 1, steps - 1),),
                         memory_space=pltpu.SMEM),
            pl.BlockSpec((tc, d), lambda i: (i, 0)),
            pl.BlockSpec((tc, LANES), lambda i: (i, 0)),
            pl.BlockSpec((1, SUBLANES, d), lambda i: (i // tpb, 0, 0)),
            _resident((1, d)),
            pl.BlockSpec(memory_space=pl.ANY),
        ],
        out_specs=pl.BlockSpec((tc, d), lambda i: (i, 0)),
        scratch_shapes=[pltpu.VMEM((2, TOP_K, tc // SUBLANES, SUBLANES, d // 2), U32),
                        pltpu.SemaphoreType.DMA((2,))],
        compiler_params=pltpu.CompilerParams(dimension_semantics=("arbitrary",),
                                             vmem_limit_bytes=VMEM_LIMIT,
                                             disable_bounds_checks=True),
        name="moe_combine",
    )(dest, dest, h, meta, modr, final_g, ys)


def _moe(h, modr, ng, w_router, b_router, layer, wg, bg, wu, bu, wd, bd, final_g, seq, final):
    n, d = h.shape
    n_exp = w_router.shape[1]
    te = EXPERT_TILE
    n_slots = n * TOP_K + n_exp * te
    nblk = n_slots // te
    u, meta, cnt = _router_call(h, modr, ng, w_router, b_router, seq)

    idx = meta[:, 0:TOP_K].astype(jnp.int32)
    rank = meta[:, TOP_K:2 * TOP_K].astype(jnp.int32)
    counts = cnt[0, :n_exp].astype(jnp.int32)
    padded = (counts + te - 1) // te * te
    pad_ends = jnp.cumsum(padded)
    pad_starts = pad_ends - padded
    dest = (pad_starts[idx] + rank).reshape(-1)
    n_used = pad_ends[-1] // te
    blk = jnp.minimum(jnp.arange(nblk, dtype=jnp.int32), n_used - 1)
    block_exp = jnp.minimum(jnp.sum(pad_ends[None, :] <= (blk * te)[:, None], axis=1),
                            n_exp - 1).astype(jnp.int32)
    within = jnp.arange(te, dtype=jnp.int32)[None, :]
    n_fill = (padded - counts)[:, None]
    fill_rows = jnp.where(within < n_fill, (pad_starts + counts)[:, None] + within, n_slots)
    pad_rows = jnp.sort(fill_rows.reshape(-1)).astype(jnp.int32)
    n_pad = jnp.sum(padded - counts).astype(jnp.int32).reshape(1)

    xs = _dispatch_call(u, dest, pad_rows, n_pad, n_slots)
    ys = _expert_call(xs, block_exp, n_used.astype(jnp.int32).reshape(1), layer,
                      wg, bg, wu, bu, wd, bd)
    return _combine_call(h, meta, modr, final_g, ys, dest, seq, final)


def kernel(x, c, ada_w, ada_b, norm_mix_g, norm_ffn_g, cv_w_in, cv_sc_w, cv_cf_w, cv_cf_b, cv_ln_g, cv_ln_b, cv_w_out, s5_lam_re, s5_lam_im, s5_log_dt, s5_b_re, s5_b_im, s5_c_re, s5_c_im, s5_d, s5_w_glu, router_w, router_b, exp_w_gate, exp_b_gate, exp_w_up, exp_b_up, exp_w_down, exp_b_down, final_g):
    bsz, seq, d = x.shape
    depth = ada_w.shape[0]
    n = bsz * seq
    assert bsz <= SUBLANES
    h = x.reshape(n, d)
    mod = _mod_call(c, ada_w, ada_b)
    fg = final_g.reshape(1, d)
    for layer in range(depth):
        modr = jnp.zeros((bsz, SUBLANES, d), F32).at[:, :6].set(mod[layer, :bsz].reshape(bsz, 6, d))
        ng = norm_mix_g[layer].reshape(1, d)
        j = layer // 2
        if layer % 2 == 0:
            h = _conv_mixer_call(h, modr, ng, cv_w_in[j], cv_sc_w[j], cv_cf_w[j], cv_cf_b[j],
                                 cv_ln_g[j], cv_ln_b[j], cv_w_out[j], bsz, seq)
        else:
            h = _s5_mixer(h, modr, ng, s5_lam_re[j], s5_lam_im[j], s5_log_dt[j], s5_b_re[j],
                          s5_b_im[j], s5_c_re[j], s5_c_im[j], s5_d[j], s5_w_glu[j], bsz, seq)
        h = _moe(h, modr, norm_ffn_g[layer].reshape(1, d), router_w[layer], router_b[layer],
                 layer, exp_w_gate, exp_b_gate, exp_w_up, exp_b_up, exp_w_down, exp_b_down, fg, seq,
                 layer == depth - 1)
    return h.reshape(bsz, seq, d)
```
